```python
import jax
import jax.numpy as jnp
from jax import lax
import numpy as np

D_MODEL = 2048
BATCH = 2
SEQ = 8192
DEPTH = 4

GRID_W = 64
CTX_LEN = 256
EPS = 1e-6

NA_HEADS = 16
NA_HEAD_DIM = 128
NA_WIDTH = NA_HEADS * NA_HEAD_DIM
NA_KH = 8
NA_KW = 16

SSD_HEAD_DIM = 64
SSD_WIDTH = D_MODEL
SSD_HEADS = SSD_WIDTH // SSD_HEAD_DIM
SSD_GROUPS = 8
SSD_HPG = SSD_HEADS // SSD_GROUPS
SSD_STATE = 128
SSD_CONV = 5
SSD_CHUNK = 128
SSD_XBC = SSD_WIDTH + 2 * SSD_GROUPS * SSD_STATE

SC_WIDTH = D_MODEL
SC_CONV = 3

MIX_WIDTH = NA_WIDTH + SSD_WIDTH
COL_Q = 0
COL_GATE_A = COL_Q + NA_WIDTH
COL_Z = COL_GATE_A + NA_WIDTH
COL_K = COL_Z + SSD_WIDTH
COL_V = COL_K + NA_WIDTH
COL_XBC = COL_V + NA_WIDTH
COL_DT = COL_XBC + SSD_XBC
EVEN_IN = COL_DT + 2 * SSD_HEADS
ODD_IN = 4 * SC_WIDTH
N_EVEN = (DEPTH + 1) // 2
N_ODD = DEPTH // 2

kernel_name = 'hybrid_natten_ssd_shortconv_dit'


def rms_norm(x, g):
    xf = x.astype(jnp.float32)
    y = xf * lax.rsqrt(jnp.mean(xf * xf, axis=-1, keepdims=True) + EPS)
    return (y * g.astype(jnp.float32)).astype(x.dtype)


def adaln(cond, w, b):
    m = jax.nn.silu(cond) @ w + b
    return jnp.split(m, 3, axis=-1)


def modulate(h, shift, scale):
    return h * (1 + scale) + shift


def cols(p, start, width, base):
    return p[..., start - base:start - base + width]


def heads(t, n, d):
    return t.reshape(*t.shape[:2], n, d)


def dw_conv(x, w, b=None):
    k = w.shape[0]
    y = lax.conv_general_dilated(x, w[:, None, :], window_strides=(1,), padding=[(k // 2, k // 2)],
                                 dimension_numbers=('NWC', 'WIO', 'NWC'), feature_group_count=x.shape[-1])
    return y if b is None else y + b


def neighbourhood_attention(q, k, v, k_ctx, v_ctx, rpb):
    b, seq, nh, dh = q.shape
    rows = seq // GRID_W
    kh = min(NA_KH, rows)
    scale = dh ** -0.5
    qg = q.reshape(b, rows, GRID_W, nh, dh)
    kg = k.reshape(b, rows, GRID_W, nh, dh)
    vg = v.reshape(b, rows, GRID_W, nh, dh)
    col = jnp.arange(GRID_W)
    col_start = jnp.clip(col - NA_KW // 2, 0, GRID_W - NA_KW)
    in_win = (col[None, :] >= col_start[:, None]) & (col[None, :] < col_start[:, None] + NA_KW)
    mask = jnp.broadcast_to(in_win[:, None, :], (GRID_W, kh, GRID_W)).reshape(GRID_W, kh * GRID_W)
    dc_idx = jnp.clip(col[None, :] - col[:, None], -(NA_KW - 1), NA_KW - 1) + NA_KW - 1

    def row_block(r):
        rs = jnp.clip(r - kh // 2, 0, rows - kh)
        qr = lax.dynamic_index_in_dim(qg, r, axis=1, keepdims=False)
        kb = lax.dynamic_slice_in_dim(kg, rs, kh, axis=1).reshape(b, kh * GRID_W, nh, dh)
        vb = lax.dynamic_slice_in_dim(vg, rs, kh, axis=1).reshape(b, kh * GRID_W, nh, dh)
        dr_idx = rs + jnp.arange(kh) - r + NA_KH - 1
        bias = rpb[:, dr_idx[None, :, None], dc_idx[:, None, :]].reshape(nh, GRID_W, kh * GRID_W)
        s_lat = jnp.einsum('bqhd,bkhd->bhqk', qr, kb).astype(jnp.float32) * scale + bias.astype(jnp.float32)
        s_lat = jnp.where(mask, s_lat, -jnp.inf)
        s_ctx = jnp.einsum('bqhd,bkhd->bhqk', qr, k_ctx).astype(jnp.float32) * scale
        pr = jax.nn.softmax(jnp.concatenate([s_lat, s_ctx], axis=-1), axis=-1).astype(v.dtype)
        return (jnp.einsum('bhqk,bkhd->bqhd', pr[..., :kh * GRID_W], vb)
                + jnp.einsum('bhqk,bkhd->bqhd', pr[..., kh * GRID_W:], v_ctx))

    out = lax.map(row_block, jnp.arange(rows))
    return jnp.moveaxis(out, 0, 1).reshape(b, seq, nh, dh)


def context_attention(q, k, v):
    s = jnp.einsum('bqhd,bkhd->bhqk', q, k).astype(jnp.float32) * q.shape[-1] ** -0.5
    pr = jax.nn.softmax(s, axis=-1).astype(v.dtype)
    return jnp.einsum('bhqk,bkhd->bqhd', pr, v)


def ssd_scan(x, dt, a, bm, cm, h0):
    b, seq = x.shape[:2]
    nc = seq // SSD_CHUNK

    def chunks(t):
        return jnp.moveaxis(t.reshape(b, nc, SSD_CHUNK, *t.shape[2:]), 1, 0)

    tri = jnp.tril(jnp.ones((SSD_CHUNK, SSD_CHUNK), bool))[None, :, :, None, None]

    def step(h, inp):
        xc, dtc, bc, cc = inp
        cs = jnp.cumsum(dtc * a, axis=1)
        seg = cs[:, :, None] - cs[:, None, :]
        decay = jnp.exp(jnp.where(tri, seg, -jnp.inf))
        cb = jnp.einsum('bign,bjgn->bijg', cc, bc)
        w = cb[..., None] * decay * dtc[:, None]
        y = jnp.einsum('bijgr,bjgrp->bigrp', w, xc)
        y = y + jnp.einsum('bign,bgrpn->bigrp', cc, h) * jnp.exp(cs)[..., None]
        to_end = jnp.exp(cs[:, -1:] - cs) * dtc
        h = h * jnp.exp(cs[:, -1])[..., None, None] + jnp.einsum('bjgn,bjgr,bjgrp->bgrpn', bc, to_end, xc)
        return h, y

    h, ys = lax.scan(step, h0, (chunks(x), chunks(dt), chunks(bm), chunks(cm)))
    return jnp.moveaxis(ys, 0, 1).reshape(x.shape), h


def ssd_final_state(x, dt, a, bm):
    cs = jnp.cumsum(dt * a, axis=1)
    to_end = jnp.exp(cs[:, -1:] - cs) * dt
    return jnp.einsum('bjgn,bjgr,bjgrp->bgrpn', bm, to_end, x)


def _flip(t, rev):
    return jnp.flip(t, axis=1) if rev else t


def bidirectional_ssd(lat, con, a, want_ctx):
    xs, dt, bm, cm = lat
    xs_c, dt_c, bm_c, cm_c = con
    b = xs.shape[0]
    y = jnp.zeros_like(xs)
    yc = jnp.zeros_like(xs_c) if want_ctx else None
    for d in range(2):
        rev = d == 1
        if want_ctx:
            h0 = jnp.zeros((b, SSD_GROUPS, SSD_HPG, SSD_HEAD_DIM, SSD_STATE), jnp.float32)
            yc_d, hc = ssd_scan(_flip(xs_c, rev), _flip(dt_c[:, :, d], rev), a[d], _flip(bm_c, rev), _flip(cm_c, rev), h0)
            yc = yc + _flip(yc_d, rev)
        else:
            hc = ssd_final_state(_flip(xs_c, rev), _flip(dt_c[:, :, d], rev), a[d], _flip(bm_c, rev))
        y_d, _ = ssd_scan(_flip(xs, rev), _flip(dt[:, :, d], rev), a[d], _flip(bm, rev), _flip(cm, rev), hc)
        y = y + _flip(y_d, rev)
    return y, yc


def ssd_inputs(p, base, conv_w, conv_b, dt_bias):
    b, seq = p.shape[:2]
    gn = SSD_GROUPS * SSD_STATE
    xbc = jax.nn.silu(dw_conv(cols(p, COL_XBC, SSD_XBC, base), conv_w, conv_b)).astype(jnp.float32)
    xs = xbc[..., :SSD_WIDTH].reshape(b, seq, SSD_GROUPS, SSD_HPG, SSD_HEAD_DIM)
    bm = xbc[..., SSD_WIDTH:SSD_WIDTH + gn].reshape(b, seq, SSD_GROUPS, SSD_STATE)
    cm = xbc[..., SSD_WIDTH + gn:].reshape(b, seq, SSD_GROUPS, SSD_STATE)
    dt_raw = cols(p, COL_DT, 2 * SSD_HEADS, base).astype(jnp.float32).reshape(b, seq, 2, SSD_GROUPS, SSD_HPG)
    dt = jax.nn.softplus(dt_raw + dt_bias.astype(jnp.float32).reshape(2, SSD_GROUPS, SSD_HPG))
    return xs, dt, bm, cm


def gated_group_rmsnorm(y, z, g):
    b, seq = y.shape[:2]
    yz = (y.reshape(b, seq, SSD_WIDTH) * jax.nn.silu(z.astype(jnp.float32))).reshape(b, seq, SSD_GROUPS, -1)
    yz = yz * lax.rsqrt(jnp.mean(yz * yz, axis=-1, keepdims=True) + EPS)
    return (yz.reshape(b, seq, SSD_WIDTH) * g.astype(jnp.float32)).astype(z.dtype)


def na_ssd_mixer(h, hc, w_in, conv_w, conv_b, a_log, dt_bias, d_skip, ssm_norm_g,
                 q_norm_g, k_norm_g, rpb, w_out, update_ctx):
    b, seq, _ = h.shape
    p = h @ w_in
    base = 0 if update_ctx else COL_K
    pc = hc @ w_in[:, base:]
    q = rms_norm(heads(cols(p, COL_Q, NA_WIDTH, 0), NA_HEADS, NA_HEAD_DIM), q_norm_g)
    k = rms_norm(heads(cols(p, COL_K, NA_WIDTH, 0), NA_HEADS, NA_HEAD_DIM), k_norm_g)
    v = heads(cols(p, COL_V, NA_WIDTH, 0), NA_HEADS, NA_HEAD_DIM)
    kc = rms_norm(heads(cols(pc, COL_K, NA_WIDTH, base), NA_HEADS, NA_HEAD_DIM), k_norm_g)
    vc = heads(cols(pc, COL_V, NA_WIDTH, base), NA_HEADS, NA_HEAD_DIM)
    ya = neighbourhood_attention(q, k, v, kc, vc, rpb).reshape(b, seq, NA_WIDTH)
    ya = ya * jax.nn.silu(cols(p, COL_GATE_A, NA_WIDTH, 0))
    a = -jnp.exp(a_log.astype(jnp.float32)).reshape(2, SSD_GROUPS, SSD_HPG)
    d = d_skip.astype(jnp.float32).reshape(SSD_GROUPS, SSD_HPG)[..., None]
    lat = ssd_inputs(p, 0, conv_w, conv_b, dt_bias)
    con = ssd_inputs(pc, base, conv_w, conv_b, dt_bias)
    y_ssd, yc_ssd = bidirectional_ssd(lat, con, a, update_ctx)
    yb = gated_group_rmsnorm(y_ssd + d * lat[0], cols(p, COL_Z, SSD_WIDTH, 0), ssm_norm_g)
    out = jnp.concatenate([ya, yb], axis=-1) @ w_out
    out_c = None
    if update_ctx:
        n_ctx = hc.shape[1]
        qc = rms_norm(heads(cols(pc, COL_Q, NA_WIDTH, 0), NA_HEADS, NA_HEAD_DIM), q_norm_g)
        yac = context_attention(qc, kc, vc).reshape(b, n_ctx, NA_WIDTH) * jax.nn.silu(cols(pc, COL_GATE_A, NA_WIDTH, 0))
        ybc = gated_group_rmsnorm(yc_ssd + d * con[0], cols(pc, COL_Z, SSD_WIDTH, 0), ssm_norm_g)
        out_c = jnp.concatenate([yac, ybc], axis=-1) @ w_out
    return out, out_c


def short_conv_mixer(h, w_in, conv_w, w_out):
    p = h @ w_in
    bg, cg, hv, g = jnp.split(p, 4, axis=-1)
    y = bg * dw_conv(cg * hv, conv_w)
    return (jax.nn.silu(g) * y) @ w_out


def setup_inputs(seed: int = 0) -> dict:
    key = jax.random.key(seed)
    ks = jax.random.split(key, 21)
    nrm = jax.random.normal
    dt0 = jnp.exp(jax.random.uniform(ks[11], (N_EVEN, 2, SSD_HEADS), minval=float(np.log(1e-3)), maxval=float(np.log(1e-1))))
    return {
        'x': nrm(ks[0], (BATCH, SEQ, D_MODEL), jnp.float32),
        'c': nrm(ks[1], (BATCH, D_MODEL), jnp.float32),
        'ctx': nrm(ks[2], (BATCH, CTX_LEN, D_MODEL), jnp.float32),
        'c_ctx': nrm(ks[3], (D_MODEL,), jnp.float32),
        'ada_w': nrm(ks[4], (DEPTH, D_MODEL, 3 * D_MODEL), jnp.float32) * (0.5 * D_MODEL ** -0.5),
        'ada_b': nrm(ks[5], (DEPTH, 3 * D_MODEL), jnp.float32) * 0.02,
        'norm_g': 1.0 + 0.05 * nrm(ks[6], (DEPTH, D_MODEL), jnp.float32),
        'na_ssd_w_in': nrm(ks[7], (N_EVEN, D_MODEL, EVEN_IN), jnp.float32) * D_MODEL ** -0.5,
        'ssd_conv_w': nrm(ks[8], (N_EVEN, SSD_CONV, SSD_XBC), jnp.float32) * SSD_CONV ** -0.5,
        'ssd_conv_b': nrm(ks[9], (N_EVEN, SSD_XBC), jnp.float32) * 0.02,
        'ssd_a_log': jnp.log(jax.random.uniform(ks[10], (N_EVEN, 2, SSD_HEADS), minval=1.0, maxval=16.0)),
        'ssd_dt_bias': dt0 + jnp.log(-jnp.expm1(-dt0)),
        'ssd_d': 1.0 + 0.1 * nrm(ks[12], (N_EVEN, SSD_HEADS), jnp.float32),
        'ssd_norm_g': 1.0 + 0.05 * nrm(ks[13], (N_EVEN, SSD_WIDTH), jnp.float32),
        'q_norm_g': 1.0 + 0.05 * nrm(ks[14], (N_EVEN, NA_HEAD_DIM), jnp.float32),
        'k_norm_g': 1.0 + 0.05 * nrm(ks[15], (N_EVEN, NA_HEAD_DIM), jnp.float32),
        'na_rpb': nrm(ks[16], (N_EVEN, NA_HEADS, 2 * NA_KH - 1, 2 * NA_KW - 1), jnp.float32) * 0.1,
        'na_ssd_w_out': nrm(ks[17], (N_EVEN, MIX_WIDTH, D_MODEL), jnp.float32) * MIX_WIDTH ** -0.5,
        'sc_w_in': nrm(ks[18], (N_ODD, D_MODEL, ODD_IN), jnp.float32) * D_MODEL ** -0.5,
        'sc_conv_w': nrm(ks[19], (N_ODD, SC_CONV, SC_WIDTH), jnp.float32) * SC_CONV ** -0.5,
        'sc_w_out': nrm(ks[20], (N_ODD, SC_WIDTH, D_MODEL), jnp.float32) * SC_WIDTH ** -0.5,
    }


def reference(x, c, ctx, c_ctx, ada_w, ada_b, norm_g, na_ssd_w_in, ssd_conv_w, ssd_conv_b,
              ssd_a_log, ssd_dt_bias, ssd_d, ssd_norm_g, q_norm_g, k_norm_g, na_rpb,
              na_ssd_w_out, sc_w_in, sc_conv_w, sc_w_out):
    for i in range(DEPTH):
        update_ctx = any(j % 2 == 0 for j in range(i + 1, DEPTH))
        needs_ctx = (i % 2 == 0) or update_ctx
        shift, scale, gate = adaln(c, ada_w[i], ada_b[i])
        h = modulate(rms_norm(x, norm_g[i]), shift[:, None], scale[:, None])
        if needs_ctx:
            shift_c, scale_c, gate_c = adaln(c_ctx, ada_w[i], ada_b[i])
            hc = modulate(rms_norm(ctx, norm_g[i]), shift_c, scale_c)
        if i % 2 == 0:
            e = i // 2
            y, yc = na_ssd_mixer(h, hc, na_ssd_w_in[e], ssd_conv_w[e], ssd_conv_b[e], ssd_a_log[e],
                                 ssd_dt_bias[e], ssd_d[e], ssd_norm_g[e], q_norm_g[e], k_norm_g[e],
                                 na_rpb[e], na_ssd_w_out[e], update_ctx)
        else:
            o = i // 2
            y = short_conv_mixer(h, sc_w_in[o], sc_conv_w[o], sc_w_out[o])
            yc = short_conv_mixer(hc, sc_w_in[o], sc_conv_w[o], sc_w_out[o]) if update_ctx else None
        x = x + gate[:, None] * y
        if update_ctx:
            ctx = ctx + gate_c * yc
    return x
```

```python
import functools

import jax
import jax.numpy as jnp
import numpy as np
from jax import lax
from jax.experimental import pallas as pl
from jax.experimental.pallas import tpu as pltpu

F32 = jnp.float32
BF16 = jnp.bfloat16

EPS = 1e-6
GRID_W = 64
NA_HEADS = 16
NA_HEAD_DIM = 128
NA_KH = 8
NA_KW = 16
SSD_HEAD_DIM = 64
SSD_GROUPS = 8
SSD_HPG = 4
SSD_HEADS = SSD_GROUPS * SSD_HPG
SSD_STATE = 128
SSD_CONV = 5
SSD_CHUNK = 128
SC_CONV = 3

V7X_LANES = 128
V7X_VMEM_LIMIT = 56 * 1024 * 1024

MASKED = -1e30

NA_QROWS = 8
NA_KROWS = NA_QROWS + NA_KH - 1


def _silu(x):
    return x * (1.0 / (1.0 + jnp.exp(-x)))


def _softplus(x):
    return jnp.maximum(x, 0.0) + jnp.log(1.0 + jnp.exp(-jnp.abs(x)))


def _split3(a):
    hi = a.astype(BF16)
    r1 = a - hi.astype(F32)
    mid = r1.astype(BF16)
    lo = (r1 - mid.astype(F32)).astype(BF16)
    return hi, mid, lo


def _params(sem, vmem=V7X_VMEM_LIMIT):
    return pltpu.CompilerParams(dimension_semantics=sem, vmem_limit_bytes=vmem)


def _adaln_kernel(cond_ref, w_ref, b_ref, o_ref):
    s = _silu(cond_ref[...])
    w = w_ref[...]
    s_hi, s_mid, _ = _split3(s)
    w_hi, w_mid, _ = _split3(w)
    acc = jnp.dot(s_hi, w_hi, preferred_element_type=F32)
    acc += jnp.dot(s_hi, w_mid, preferred_element_type=F32)
    acc += jnp.dot(s_mid, w_hi, preferred_element_type=F32)
    o_ref[...] = acc + b_ref[...]


def adaln_all(cond, ada_w, ada_b, tn=768):
    depth, d, n = ada_w.shape
    rows = cond.shape[0]
    return pl.pallas_call(
        _adaln_kernel,
        grid=(depth, n // tn),
        in_specs=[
            pl.BlockSpec((rows, d), lambda l, j: (0, 0)),
            pl.BlockSpec((None, d, tn), lambda l, j: (l, 0, j)),
            pl.BlockSpec((None, 1, tn), lambda l, j: (l, 0, j)),
        ],
        out_specs=pl.BlockSpec((None, rows, tn), lambda l, j: (l, 0, j)),
        out_shape=jax.ShapeDtypeStruct((depth, rows, n), F32),
        compiler_params=_params(("parallel", "parallel")),
        name="adaln",
    )(cond, ada_w, ada_b.reshape(depth, 1, n))


_PROJ_ROWS = 128


def _proj_kernel(*refs, with_dt):
    if with_dt:
        x_ref, shift_ref, scale_ref, g_ref, w_ref, wdt_ref, o_ref, odt_ref, h_scr = refs
    else:
        x_ref, shift_ref, scale_ref, g_ref, w_ref, o_ref, h_scr = refs
    tm = x_ref.shape[0]

    @pl.when(pl.program_id(1) == 0)
    def _():
        gain = g_ref[...] * (1.0 + scale_ref[...])
        shift = shift_ref[...]

        def body(r, carry):
            rows = pl.ds(pl.multiple_of(r * _PROJ_ROWS, _PROJ_ROWS), _PROJ_ROWS)
            x = x_ref[rows, :]
            ms = jnp.mean(x * x, axis=-1, keepdims=True)
            h_scr[rows, :] = (x * lax.rsqrt(ms + EPS) * gain + shift).astype(BF16)
            return carry

        lax.fori_loop(0, tm // _PROJ_ROWS, body, 0)
        if with_dt:
            odt_ref[...] = lax.dot_general(wdt_ref[...], h_scr[...], (((1,), (1,)), ((), ())),
                                           preferred_element_type=F32)

    o_ref[...] = jnp.dot(h_scr[...], w_ref[...], preferred_element_type=F32).astype(o_ref.dtype)


def project(x2, mod, layer, rows_per_seq, mod_row, g, w, wdt_t=None, *, tm, tn):
    m, d = x2.shape
    n = w.shape[1]
    tm = min(tm, rows_per_seq)
    tiles_per_seq = rows_per_seq // tm
    if mod_row is None:
        row = lambda i: i // tiles_per_seq
    else:
        row = lambda i: mod_row
    with_dt = wdt_t is not None
    in_specs = [
        pl.BlockSpec((tm, d), lambda i, j: (i, 0)),
        pl.BlockSpec((None, None, 1, d), lambda i, j: (layer, row(i), 0, 0)),
        pl.BlockSpec((None, None, 1, d), lambda i, j: (layer, row(i), 0, 1)),
        pl.BlockSpec((1, d), lambda i, j: (0, 0)),
        pl.BlockSpec((d, tn), lambda i, j: (0, j)),
    ]
    out_specs = [pl.BlockSpec((tm, tn), lambda i, j: (i, j))]
    out_shape = [jax.ShapeDtypeStruct((m, n), BF16)]
    args = [x2, mod, mod, g, w]
    if with_dt:
        in_specs.append(pl.BlockSpec((wdt_t.shape[0], d), lambda i, j: (0, 0)))
        out_specs.append(pl.BlockSpec((wdt_t.shape[0], tm), lambda i, j: (0, i)))
        out_shape.append(jax.ShapeDtypeStruct((wdt_t.shape[0], m), F32))
        args.append(wdt_t)
    outs = pl.pallas_call(
        functools.partial(_proj_kernel, with_dt=with_dt),
        grid=(m // tm, n // tn),
        in_specs=in_specs,
        out_specs=out_specs,
        out_shape=out_shape,
        scratch_shapes=[pltpu.VMEM((tm, d), BF16)],
        compiler_params=_params(("parallel", "arbitrary")),
        name="proj_dt" if with_dt else "proj",
    )(*args)
    return outs if with_dt else outs[0]


def _bias_kernel(rpb_ref, o_ref):
    h = pl.program_id(0)
    qc = lax.broadcasted_iota(jnp.int32, (GRID_W, GRID_W), 0)
    kc = lax.broadcasted_iota(jnp.int32, (GRID_W, GRID_W), 1)
    start = jnp.clip(qc - NA_KW // 2, 0, GRID_W - NA_KW)
    in_win = (kc >= start) & (kc < start + NA_KW)
    dc = jnp.clip(kc - qc, -(NA_KW - 1), NA_KW - 1) + NA_KW - 1
    for dr in range(2 * NA_KH - 1):
        acc = jnp.zeros((GRID_W, GRID_W), F32)
        for j in range(2 * NA_KW - 1):
            acc = jnp.where(dc == j, rpb_ref[h, dr * (2 * NA_KW - 1) + j], acc)
        o_ref[dr] = jnp.where(in_win, acc, MASKED)


def expand_bias(rpb):
    nh = rpb.shape[0]
    ndr, ndc = 2 * NA_KH - 1, 2 * NA_KW - 1
    return pl.pallas_call(
        _bias_kernel,
        grid=(nh,),
        in_specs=[pl.BlockSpec(memory_space=pltpu.SMEM)],
        out_specs=pl.BlockSpec((None, ndr, GRID_W, GRID_W), lambda h: (h, 0, 0, 0)),
        out_shape=jax.ShapeDtypeStruct((nh, ndr, GRID_W, GRID_W), F32),
        compiler_params=_params(("arbitrary",)),
        name="rpb_expand",
    )(rpb.reshape(nh, ndr * ndc))


def _head_norm(x, g):
    xf = x.astype(F32)
    return xf * lax.rsqrt(jnp.mean(xf * xf, axis=-1, keepdims=True) + EPS) * g


def _window_rows(case, qr, n_rows):
    if case == 1:
        lo, dr0 = qr, NA_KH - 1 - NA_KH // 2
    elif case == 0:
        lo = max(qr - NA_KH // 2, 0)
        dr0 = lo - qr + NA_KH - 1
    else:
        r = n_rows - NA_QROWS + qr
        rs = min(r - NA_KH // 2, n_rows - NA_KH)
        lo = rs - (n_rows - NA_KROWS)
        dr0 = rs - r + NA_KH - 1
    return lo, dr0


def _na_kernel(q_ref, gate_ref, k_ref, v_ref, kc_ref, vc_ref, cmat_ref, qg_ref, kg_ref, o_ref,
               kn_scr, kcn_scr, bias_scr, *, n_rows):
    b = pl.program_id(1)
    i = pl.program_id(2)
    nb = pl.num_programs(2)
    seq = k_ref.shape[0]
    nq = NA_QROWS * GRID_W
    nk = NA_KROWS * GRID_W
    kg = kg_ref[...]

    @pl.when((b == 0) & (i == 0))
    def _():
        masked = jnp.full((GRID_W, GRID_W), MASKED, F32)
        for case in range(3):
            for qr in range(NA_QROWS):
                lo, dr0 = _window_rows(case, qr, n_rows)
                strip = [cmat_ref[dr0 + kr - lo] if lo <= kr < lo + NA_KH else masked
                         for kr in range(NA_KROWS)]
                bias_scr[case, qr * GRID_W:(qr + 1) * GRID_W, :] = jnp.concatenate(strip, axis=1)

    @pl.when(i == 0)
    def _():
        def body(c, carry):
            rows = pl.ds(pl.multiple_of(c * 512, 512), 512)
            kn_scr[rows, :] = _head_norm(k_ref[rows, :], kg).astype(BF16)
            return carry

        lax.fori_loop(0, seq // 512, body, 0)
        kcn_scr[...] = _head_norm(kc_ref[...], kg).astype(BF16)

    scale = NA_HEAD_DIM ** -0.5
    qn = (_head_norm(q_ref[...], qg_ref[...]) * scale).astype(BF16)
    kstart = jnp.clip(i * NA_QROWS - NA_KH // 2, 0, n_rows - NA_KROWS)
    krows = pl.ds(pl.multiple_of(kstart * GRID_W, GRID_W), nk)
    case = jnp.where(i == 0, 0, jnp.where(i == nb - 1, 2, 1))
    nt = (((1,), (1,)), ((), ()))
    s1 = lax.dot_general(qn, kn_scr[krows, :], nt, preferred_element_type=F32) + bias_scr[case]
    s2 = lax.dot_general(qn, kcn_scr[...], nt, preferred_element_type=F32)
    m = jnp.maximum(jnp.max(s1, axis=-1, keepdims=True), jnp.max(s2, axis=-1, keepdims=True))
    p1 = jnp.exp(s1 - m)
    p2 = jnp.exp(s2 - m)
    l = jnp.sum(p1, axis=-1, keepdims=True) + jnp.sum(p2, axis=-1, keepdims=True)
    o = jnp.dot(p1.astype(BF16), v_ref[krows, :], preferred_element_type=F32)
    o += jnp.dot(p2.astype(BF16), vc_ref[...], preferred_element_type=F32)
    o_ref[...] = (o * (1.0 / l) * _silu(gate_ref[...].astype(F32))).astype(o_ref.dtype)


def neighbourhood_attention(p, pc, cmat, qg, kg, cols):
    bsz, seq, _ = p.shape
    nctx = pc.shape[1]
    n_rows = seq // GRID_W
    assert n_rows >= NA_KROWS and n_rows % NA_QROWS == 0
    nb = n_rows // NA_QROWS
    nq = NA_QROWS * GRID_W
    dh = NA_HEAD_DIM
    cq, cg, ck, cv = (cols[k] // dh for k in ("q", "gate", "k", "v"))
    return pl.pallas_call(
        functools.partial(_na_kernel, n_rows=n_rows),
        grid=(NA_HEADS, bsz, nb),
        in_specs=[
            pl.BlockSpec((None, nq, dh), lambda h, b, i: (b, i, cq + h)),
            pl.BlockSpec((None, nq, dh), lambda h, b, i: (b, i, cg + h)),
            pl.BlockSpec((None, seq, dh), lambda h, b, i: (b, 0, ck + h)),
            pl.BlockSpec((None, seq, dh), lambda h, b, i: (b, 0, cv + h)),
            pl.BlockSpec((None, nctx, dh), lambda h, b, i: (b, 0, ck + h)),
            pl.BlockSpec((None, nctx, dh), lambda h, b, i: (b, 0, cv + h)),
            pl.BlockSpec((None, 2 * NA_KH - 1, GRID_W, GRID_W), lambda h, b, i: (h, 0, 0, 0)),
            pl.BlockSpec((1, dh), lambda h, b, i: (0, 0)),
            pl.BlockSpec((1, dh), lambda h, b, i: (0, 0)),
        ],
        out_specs=pl.BlockSpec((None, nq, dh), lambda h, b, i: (b, i, h)),
        out_shape=jax.ShapeDtypeStruct((bsz, seq, NA_HEADS * dh), BF16),
        scratch_shapes=[
            pltpu.VMEM((seq, dh), BF16),
            pltpu.VMEM((nctx, dh), BF16),
            pltpu.VMEM((3, nq, NA_KROWS * GRID_W), F32),
        ],
        compiler_params=_params(("arbitrary", "arbitrary", "arbitrary")),
        name="nbr_attention",
    )(p, p, p, p, pc, pc, cmat, qg, kg)


def _ctx_attn_kernel(q_ref, gate_ref, k_ref, v_ref, qg_ref, kg_ref, o_ref):
    scale = NA_HEAD_DIM ** -0.5
    qn = (_head_norm(q_ref[...], qg_ref[...]) * scale).astype(BF16)
    kn = _head_norm(k_ref[...], kg_ref[...]).astype(BF16)
    s = lax.dot_general(qn, kn, (((1,), (1,)), ((), ())), preferred_element_type=F32)
    m = jnp.max(s, axis=-1, keepdims=True)
    pr = jnp.exp(s - m)
    l = jnp.sum(pr, axis=-1, keepdims=True)
    o = jnp.dot(pr.astype(BF16), v_ref[...], preferred_element_type=F32)
    o_ref[...] = (o * (1.0 / l) * _silu(gate_ref[...].astype(F32))).astype(o_ref.dtype)


def context_attention(pc, qg, kg, cols):
    bsz, nctx, _ = pc.shape
    dh = NA_HEAD_DIM
    cq, cg, ck, cv = (cols[k] // dh for k in ("q", "gate", "k", "v"))
    blk = lambda c: pl.BlockSpec((None, nctx, dh), lambda b, h: (b, 0, c + h))
    return pl.pallas_call(
        _ctx_attn_kernel,
        grid=(bsz, NA_HEADS),
        in_specs=[blk(cq), blk(cg), blk(ck), blk(cv),
                  pl.BlockSpec((1, dh), lambda b, h: (0, 0)),
                  pl.BlockSpec((1, dh), lambda b, h: (0, 0))],
        out_specs=pl.BlockSpec((None, nctx, dh), lambda b, h: (b, 0, h)),
        out_shape=jax.ShapeDtypeStruct((bsz, nctx, NA_HEADS * dh), BF16),
        compiler_params=_params(("parallel", "parallel")),
        name="ctx_attention",
    )(pc, pc, pc, pc, qg, kg)


_CONV_HALO = 16


def _conv_silu_chunk(src_ref, c, nc, w_ref, b_ref, stage_ref):
    t = SSD_CHUNK
    hl = _CONV_HALO
    length = src_ref.shape[0]
    start = pl.multiple_of(c * t, t)
    prev = src_ref[pl.ds(pl.multiple_of(jnp.maximum(start - hl, 0), hl), hl), :].astype(F32)
    nxt = src_ref[pl.ds(pl.multiple_of(jnp.minimum(start + t, length - hl), hl), hl), :].astype(F32)
    stage_ref[0:hl, :] = jnp.where(c > 0, prev, 0.0)
    stage_ref[hl:hl + t, :] = src_ref[pl.ds(start, t), :].astype(F32)
    stage_ref[hl + t:hl + t + hl, :] = jnp.where(c < nc - 1, nxt, 0.0)
    k = w_ref.shape[0]
    acc = jnp.zeros((t, src_ref.shape[1]), F32) + b_ref[...]
    for j in range(k):
        off = hl - k // 2 + j
        acc += stage_ref[off:off + t, :] * w_ref[j:j + 1, :]
    return _silu(acc)


def _ssd_kernel(x_ref, bm_ref, cm_ref, z_ref, dtt_ref, wx_ref, bx_ref, wb_ref, bb_ref, wc_ref, bc_ref,
                dtb_ref, a_ref, dskip_ref, ng_ref, h0_ref, y_ref, hfin_ref,
                xs_scr, bmc_scr, cmc_scr, yf_scr, stx_scr, stb_scr):
    t = SSD_CHUNK
    length = x_ref.shape[0]
    nc = length // t
    hp = SSD_HPG * SSD_HEAD_DIM

    def conv_body(c, carry):
        rows = pl.ds(pl.multiple_of(c * t, t), t)
        xs_scr[rows, :] = _conv_silu_chunk(x_ref, c, nc, wx_ref, bx_ref, stx_scr).astype(BF16)
        bmc_scr[rows, :] = _conv_silu_chunk(bm_ref, c, nc, wb_ref, bb_ref, stb_scr).astype(BF16)
        cmc_scr[rows, :] = _conv_silu_chunk(cm_ref, c, nc, wc_ref, bc_ref, stb_scr).astype(BF16)
        return carry

    lax.fori_loop(0, nc, conv_body, 0)
    hfin_ref[...] = h0_ref[...]

    ii = lax.broadcasted_iota(jnp.int32, (t, t), 0)
    jj = lax.broadcasted_iota(jnp.int32, (t, t), 1)
    lower = ii >= jj
    upper = ii <= jj
    indicator = lambda cond: jnp.where(cond, 1.0, 0.0).astype(BF16)
    lower_b = indicator(lower)
    upper_b = indicator(upper)
    eye_b = indicator(ii == jj)
    er = lax.broadcasted_iota(jnp.int32, (2 * SSD_HPG, hp), 0)
    el = lax.broadcasted_iota(jnp.int32, (2 * SSD_HPG, hp), 1) // SSD_HEAD_DIM
    expand = [indicator(er == el + d * SSD_HPG) for d in range(2)]
    lane_head = lax.broadcasted_iota(jnp.int32, (t, hp), 1) // SSD_HEAD_DIM
    head_lanes = [indicator(lane_head == r) for r in range(SSD_HPG)]
    nt = (((1,), (1,)), ((), ()))
    tn = (((0,), (0,)), ((), ()))

    def chunk(c, d):
        rows = pl.ds(pl.multiple_of(c * t, t), t)
        dt8 = _softplus(dtt_ref[:, rows] + dtb_ref[...])
        dta8 = dt8 * a_ref[...]
        tri_row, tri_col, mask = (upper_b, lower_b, lower) if d == 0 else (lower_b, upper_b, upper)
        parts = _split3(dta8)
        cs_row = sum(jnp.dot(pt, tri_row, preferred_element_type=F32) for pt in parts)
        cs_col = sum(lax.dot_general(tri_col, pt, nt, preferred_element_type=F32) for pt in parts)
        dt_col = lax.dot_general(eye_b, dt8.astype(BF16), nt, preferred_element_type=F32)
        edge = t - 1 if d == 0 else 0
        to_end_col = jnp.exp(cs_col[edge:edge + 1, :] - cs_col) * dt_col
        ecs = jnp.dot(jnp.exp(cs_col).astype(BF16), expand[d], preferred_element_type=F32)
        to_end = jnp.dot(to_end_col.astype(BF16), expand[d], preferred_element_type=F32)

        xs = xs_scr[rows, :]
        bmc = bmc_scr[rows, :]
        cmc = cmc_scr[rows, :]
        cb = lax.dot_general(cmc, bmc, nt, preferred_element_type=F32)
        ws, xbd = [], []
        for r in range(SSD_HPG):
            hrow = d * SSD_HPG + r
            seg = cs_col[:, hrow:hrow + 1] - cs_row[hrow:hrow + 1, :]
            w = cb * jnp.exp(jnp.where(mask, seg, MASKED)) * dt8[hrow:hrow + 1, :]
            ws.append(w.astype(BF16))
            xbd.append(xs * head_lanes[r])
        y = jnp.dot(jnp.concatenate(ws, axis=1), jnp.concatenate(xbd, axis=0), preferred_element_type=F32)
        h = hfin_ref[d]
        y += jnp.dot(cmc, h.astype(BF16), preferred_element_type=F32) * ecs
        xw = (xs.astype(F32) * to_end).astype(BF16)
        hfin_ref[d] = h * ecs[edge:edge + 1, :] + lax.dot_general(bmc, xw, tn, preferred_element_type=F32)
        return rows, xs, y

    def fwd_body(c, carry):
        rows, _, y = chunk(c, 0)
        yf_scr[rows, :] = y
        return carry

    lax.fori_loop(0, nc, fwd_body, 0)

    def bwd_body(k, carry):
        rows, xs, y = chunk(nc - 1 - k, 1)
        y = y + yf_scr[rows, :] + dskip_ref[...] * xs.astype(F32)
        yz = y * _silu(z_ref[rows, :].astype(F32))
        yz = yz * lax.rsqrt(jnp.mean(yz * yz, axis=-1, keepdims=True) + EPS)
        y_ref[rows, :] = (yz * ng_ref[...]).astype(y_ref.dtype)
        return carry

    lax.fori_loop(0, nc, bwd_body, 0)


def ssd_mixer(p, dt_t, conv_w, conv_b, dtb8, a8, dskip, norm_g, h0, cols):
    bsz, length, _ = p.shape
    hp = SSD_HPG * SSD_HEAD_DIM
    ns = SSD_STATE
    width = SSD_GROUPS * hp
    k = conv_w.shape[0]
    cx = cols["xbc"] // hp
    cb_ = (cols["xbc"] + width) // ns
    cc_ = (cols["xbc"] + width + SSD_GROUPS * ns) // ns
    cz = cols["z"] // hp
    st_rows = SSD_CHUNK + 2 * _CONV_HALO
    return pl.pallas_call(
        _ssd_kernel,
        grid=(bsz, SSD_GROUPS),
        in_specs=[
            pl.BlockSpec((None, length, hp), lambda b, g: (b, 0, cx + g)),
            pl.BlockSpec((None, length, ns), lambda b, g: (b, 0, cb_ + g)),
            pl.BlockSpec((None, length, ns), lambda b, g: (b, 0, cc_ + g)),
            pl.BlockSpec((None, length, hp), lambda b, g: (b, 0, cz + g)),
            pl.BlockSpec((2 * SSD_HPG, length), lambda b, g: (g, b)),
            pl.BlockSpec((k, hp), lambda b, g: (0, g)),
            pl.BlockSpec((1, hp), lambda b, g: (0, g)),
            pl.BlockSpec((k, ns), lambda b, g: (0, width // ns + g)),
            pl.BlockSpec((1, ns), lambda b, g: (0, width // ns + g)),
            pl.BlockSpec((k, ns), lambda b, g: (0, width // ns + SSD_GROUPS + g)),
            pl.BlockSpec((1, ns), lambda b, g: (0, width // ns + SSD_GROUPS + g)),
            pl.BlockSpec((None, 2 * SSD_HPG, V7X_LANES), lambda b, g: (g, 0, 0)),
            pl.BlockSpec((None, 2 * SSD_HPG, V7X_LANES), lambda b, g: (g, 0, 0)),
            pl.BlockSpec((1, hp), lambda b, g: (0, g)),
            pl.BlockSpec((1, hp), lambda b, g: (0, g)),
            pl.BlockSpec((None, None, 2, ns, hp), lambda b, g: (b, g, 0, 0, 0)),
        ],
        out_specs=[
            pl.BlockSpec((None, length, hp), lambda b, g: (b, 0, g)),
            pl.BlockSpec((None, None, 2, ns, hp), lambda b, g: (b, g, 0, 0, 0)),
        ],
        out_shape=[
            jax.ShapeDtypeStruct((bsz, length, width), BF16),
            jax.ShapeDtypeStruct((bsz, SSD_GROUPS, 2, ns, hp), F32),
        ],
        scratch_shapes=[
            pltpu.VMEM((length, hp), BF16),
            pltpu.VMEM((length, ns), BF16),
            pltpu.VMEM((length, ns), BF16),
            pltpu.VMEM((length, hp), F32),
            pltpu.VMEM((st_rows, hp), F32),
            pltpu.VMEM((st_rows, ns), F32),
        ],
        compiler_params=_params(("parallel", "parallel")),
        name="ssd_mixer",
    )(p, p, p, p, dt_t, conv_w, conv_b, conv_w, conv_b, conv_w, conv_b, dtb8, a8, dskip, norm_g, h0)


def _out_even_kernel(ya_ref, yb_ref, wa_ref, wb_ref, x_ref, gate_ref, o_ref):
    acc = jnp.dot(ya_ref[...], wa_ref[...], preferred_element_type=F32)
    acc += jnp.dot(yb_ref[...], wb_ref[...], preferred_element_type=F32)
    o_ref[...] = x_ref[...] + gate_ref[...] * acc


def out_project_even(ya, yb, w_out, x2, mod, layer, rows_per_seq, mod_row, *, tm, tn):
    m, d = x2.shape
    ka = ya.shape[1]
    kb = yb.shape[1]
    tm = min(tm, rows_per_seq)
    tiles_per_seq = rows_per_seq // tm
    row = (lambda i: i // tiles_per_seq) if mod_row is None else (lambda i: mod_row)
    return pl.pallas_call(
        _out_even_kernel,
        grid=(m // tm, d // tn),
        in_specs=[
            pl.BlockSpec((tm, ka), lambda i, j: (i, 0)),
            pl.BlockSpec((tm, kb), lambda i, j: (i, 0)),
            pl.BlockSpec((ka, tn), lambda i, j: (0, j)),
            pl.BlockSpec((kb, tn), lambda i, j: (ka // kb, j)),
            pl.BlockSpec((tm, tn), lambda i, j: (i, j)),
            pl.BlockSpec((None, None, 1, tn), lambda i, j: (layer, row(i), 0, 2 * (d // tn) + j)),
        ],
        out_specs=pl.BlockSpec((tm, tn), lambda i, j: (i, j)),
        out_shape=jax.ShapeDtypeStruct((m, d), F32),
        compiler_params=_params(("parallel", "arbitrary")),
        name="out_proj_even",
    )(ya, yb, w_out, w_out, x2, mod)


_SC_ROWS = 64
_SC_HALO = 8


def _out_odd_kernel(bg_ref, cg_ref, hv_ref, g_ref, cgp_ref, hvp_ref, cgn_ref, hvn_ref, cw_ref, w_ref,
                    x_ref, gate_ref, o_ref, u_scr, z_scr, *, tiles_per_seq):
    tm = bg_ref.shape[0]
    hl = _SC_HALO

    @pl.when(pl.program_id(1) == 0)
    def _():
        t_in_seq = pl.program_id(0) % tiles_per_seq
        up = cgp_ref[...].astype(F32) * hvp_ref[...].astype(F32)
        un = cgn_ref[...].astype(F32) * hvn_ref[...].astype(F32)
        u_scr[0:hl, :] = jnp.where(t_in_seq > 0, up, 0.0)
        u_scr[hl + tm:hl + tm + hl, :] = jnp.where(t_in_seq < tiles_per_seq - 1, un, 0.0)
        for r in range(tm // _SC_ROWS):
            rows = slice(r * _SC_ROWS, (r + 1) * _SC_ROWS)
            u_scr[hl + r * _SC_ROWS:hl + (r + 1) * _SC_ROWS, :] = (
                cg_ref[rows, :].astype(F32) * hv_ref[rows, :].astype(F32))
        k = cw_ref.shape[0]
        for r in range(tm // _SC_ROWS):
            rows = slice(r * _SC_ROWS, (r + 1) * _SC_ROWS)
            acc = jnp.zeros((_SC_ROWS, u_scr.shape[1]), F32)
            for j in range(k):
                off = hl - k // 2 + j + r * _SC_ROWS
                acc += u_scr[off:off + _SC_ROWS, :] * cw_ref[j:j + 1, :]
            z = _silu(g_ref[rows, :].astype(F32)) * bg_ref[rows, :].astype(F32) * acc
            z_scr[rows, :] = z.astype(BF16)

    acc = jnp.dot(z_scr[...], w_ref[...], preferred_element_type=F32)
    o_ref[...] = x_ref[...] + gate_ref[...] * acc


def out_project_odd(p, conv_w, w_out, x2, mod, layer, rows_per_seq, mod_row, *, tm, tn):
    m, d = x2.shape
    w = p.shape[1] // 4
    tm = min(tm, rows_per_seq)
    tiles_per_seq = rows_per_seq // tm
    row = (lambda i: i // tiles_per_seq) if mod_row is None else (lambda i: mod_row)
    hb = tm // _SC_HALO
    last = m // _SC_HALO - 1
    prev = lambda c: pl.BlockSpec((_SC_HALO, w), lambda i, j: (jnp.maximum(i * hb - 1, 0), c))
    nxt = lambda c: pl.BlockSpec((_SC_HALO, w), lambda i, j: (jnp.minimum((i + 1) * hb, last), c))
    main = lambda c: pl.BlockSpec((tm, w), lambda i, j: (i, c))
    return pl.pallas_call(
        functools.partial(_out_odd_kernel, tiles_per_seq=tiles_per_seq),
        grid=(m // tm, d // tn),
        in_specs=[
            main(0), main(1), main(2), main(3), prev(1), prev(2), nxt(1), nxt(2),
            pl.BlockSpec(conv_w.shape, lambda i, j: (0, 0)),
            pl.BlockSpec((w, tn), lambda i, j: (0, j)),
            pl.BlockSpec((tm, tn), lambda i, j: (i, j)),
            pl.BlockSpec((None, None, 1, tn), lambda i, j: (layer, row(i), 0, 2 * (d // tn) + j)),
        ],
        out_specs=pl.BlockSpec((tm, tn), lambda i, j: (i, j)),
        out_shape=jax.ShapeDtypeStruct((m, d), F32),
        scratch_shapes=[pltpu.VMEM((tm + 2 * _SC_HALO, w), F32), pltpu.VMEM((tm, w), BF16)],
        compiler_params=_params(("parallel", "arbitrary")),
        name="out_proj_odd",
    )(p, p, p, p, p, p, p, p, conv_w, w_out, x2, mod)


def _even_columns(d_model):
    na_width = NA_HEADS * NA_HEAD_DIM
    ssd_width = SSD_GROUPS * SSD_HPG * SSD_HEAD_DIM
    cols = {"q": 0}
    cols["gate"] = cols["q"] + na_width
    cols["z"] = cols["gate"] + na_width
    cols["k"] = cols["z"] + ssd_width
    cols["v"] = cols["k"] + na_width
    cols["xbc"] = cols["v"] + na_width
    cols["dt"] = cols["xbc"] + ssd_width + 2 * SSD_GROUPS * SSD_STATE
    return cols


def _dt_weight_rows(w_in, cols):
    wdt = w_in[:, cols["dt"]:cols["dt"] + 2 * SSD_HEADS]
    wdt = wdt.reshape(-1, 2, SSD_GROUPS, SSD_HPG).transpose(2, 1, 3, 0).reshape(2 * SSD_HEADS, -1)
    pad = jnp.zeros((V7X_LANES - 2 * SSD_HEADS, wdt.shape[1]), wdt.dtype)
    return jnp.concatenate([wdt, pad], axis=0).astype(BF16)


def _per_group_rows(v):
    v = v.astype(F32).reshape(2, SSD_GROUPS, SSD_HPG).transpose(1, 0, 2).reshape(SSD_GROUPS, 2 * SSD_HPG, 1)
    return jnp.broadcast_to(v, (SSD_GROUPS, 2 * SSD_HPG, V7X_LANES))


def _forward(x, c, ctx, c_ctx, ada_w, ada_b, norm_g, na_ssd_w_in, ssd_conv_w, ssd_conv_b, ssd_a_log,
             ssd_dt_bias, ssd_d, ssd_norm_g, q_norm_g, k_norm_g, na_rpb, na_ssd_w_out, sc_w_in,
             sc_conv_w, sc_w_out, *, tm, tn, tm_odd):
    bsz, seq, d = x.shape
    nctx = ctx.shape[1]
    depth = ada_w.shape[0]
    cols = _even_columns(d)
    hp = SSD_HPG * SSD_HEAD_DIM

    cond = jnp.concatenate([c, c_ctx[None, :], jnp.zeros((8 - bsz - 1, d), F32)], axis=0)
    mod = adaln_all(cond, ada_w, ada_b).reshape(depth, 8, 1, 3 * d)
    ctx_row = bsz

    x2 = x.reshape(bsz * seq, d)
    ctx2 = ctx.reshape(bsz * nctx, d)
    for i in range(depth):
        update_ctx = any(j % 2 == 0 for j in range(i + 1, depth))
        needs_ctx = (i % 2 == 0) or update_ctx
        g = norm_g[i].reshape(1, d)
        if i % 2 == 0:
            e = i // 2
            w_in = na_ssd_w_in[e]
            w_main = w_in[:, :cols["dt"]].astype(BF16)
            wdt_t = _dt_weight_rows(w_in, cols)
            w_out = na_ssd_w_out[e].astype(BF16)
            p, dt_t = project(x2, mod, i, seq, None, g, w_main, wdt_t, tm=tm, tn=tn)
            pc, dtc_t = project(ctx2, mod, i, nctx, ctx_row, g, w_main, wdt_t, tm=tm, tn=tn)
            p3 = p.reshape(bsz, seq, -1)
            pc3 = pc.reshape(bsz, nctx, -1)
            qg = q_norm_g[e].reshape(1, -1)
            kg = k_norm_g[e].reshape(1, -1)
            cmat = expand_bias(na_rpb[e])
            ya = neighbourhood_attention(p3, pc3, cmat, qg, kg, cols)
            dtb8 = _per_group_rows(ssd_dt_bias[e])
            a8 = _per_group_rows(-jnp.exp(ssd_a_log[e].astype(F32)))
            dskip = jnp.repeat(ssd_d[e].astype(F32), SSD_HEAD_DIM).reshape(1, -1)
            sng = ssd_norm_g[e].reshape(1, -1)
            cw = ssd_conv_w[e]
            cb = ssd_conv_b[e].reshape(1, -1)
            h_zero = jnp.zeros((bsz, SSD_GROUPS, 2, SSD_STATE, hp), F32)
            ybc, h_ctx = ssd_mixer(pc3, dtc_t, cw, cb, dtb8, a8, dskip, sng, h_zero, cols)
            yb, _ = ssd_mixer(p3, dt_t, cw, cb, dtb8, a8, dskip, sng, h_ctx, cols)
            x2_new = out_project_even(ya.reshape(bsz * seq, -1), yb.reshape(bsz * seq, -1), w_out, x2,
                                      mod, i, seq, None, tm=tm, tn=tn)
            if update_ctx:
                yac = context_attention(pc3, qg, kg, cols)
                ctx2 = out_project_even(yac.reshape(bsz * nctx, -1), ybc.reshape(bsz * nctx, -1), w_out,
                                        ctx2, mod, i, nctx, ctx_row, tm=tm, tn=tn)
            x2 = x2_new
        else:
            o = i // 2
            w_in = sc_w_in[o].astype(BF16)
            w_out = sc_w_out[o].astype(BF16)
            p = project(x2, mod, i, seq, None, g, w_in, tm=tm, tn=tn)
            x2_new = out_project_odd(p, sc_conv_w[o], w_out, x2, mod, i, seq, None, tm=tm_odd, tn=tn)
            if update_ctx:
                pc = project(ctx2, mod, i, nctx, ctx_row, g, w_in, tm=tm, tn=tn)
                ctx2 = out_project_odd(pc, sc_conv_w[o], w_out, ctx2, mod, i, nctx, ctx_row,
                                       tm=tm_odd, tn=tn)
            x2 = x2_new
    return x2.reshape(bsz, seq, d)


def kernel(x, c, ctx, c_ctx, ada_w, ada_b, norm_g, na_ssd_w_in, ssd_conv_w, ssd_conv_b, ssd_a_log,
           ssd_dt_bias, ssd_d, ssd_norm_g, q_norm_g, k_norm_g, na_rpb, na_ssd_w_out, sc_w_in,
           sc_conv_w, sc_w_out):
    return _forward(x, c, ctx, c_ctx, ada_w, ada_b, norm_g, na_ssd_w_in, ssd_conv_w, ssd_conv_b,
                    ssd_a_log, ssd_dt_bias, ssd_d, ssd_norm_g, q_norm_g, k_norm_g, na_rpb,
                    na_ssd_w_out, sc_w_in, sc_conv_w, sc_w_out, tm=1024, tn=512, tm_odd=512)
```

```python
import functools

import jax
import jax.numpy as jnp
import numpy as np
from jax import lax
from jax.experimental import pallas as pl
from jax.experimental.pallas import tpu as pltpu

F32 = jnp.float32
BF16 = jnp.bfloat16

EPS = 1e-6
GRID_W = 64
NA_HEADS = 16
NA_HEAD_DIM = 128
NA_KH = 8
NA_KW = 16
SSD_HEAD_DIM = 64
SSD_GROUPS = 8
SSD_HPG = 4
SSD_HEADS = SSD_GROUPS * SSD_HPG
SSD_STATE = 128
SSD_CONV = 5
SSD_CHUNK = 128
SC_CONV = 3

V7X_LANES = 128
V7X_VMEM_LIMIT = 56 * 1024 * 1024

MASKED = -1e30

NA_QROWS = 8
NA_KROWS = 16
_NA_SOFTMAX_ROWS = 32


def _silu(x):
    return x * (1.0 / (1.0 + jnp.exp(-x)))


def _softplus(x):
    return jnp.maximum(x, 0.0) + jnp.log(1.0 + jnp.exp(-jnp.abs(x)))


def _split3(a):
    hi = a.astype(BF16)
    r1 = a - hi.astype(F32)
    mid = r1.astype(BF16)
    lo = (r1 - mid.astype(F32)).astype(BF16)
    return hi, mid, lo


def _params(sem, vmem=V7X_VMEM_LIMIT):
    return pltpu.CompilerParams(dimension_semantics=sem, vmem_limit_bytes=vmem)


def _adaln_kernel(cond_ref, w_ref, b_ref, o_ref):
    s = _silu(cond_ref[...])
    w = w_ref[...]
    s_hi, s_mid, _ = _split3(s)
    w_hi, w_mid, _ = _split3(w)
    acc = jnp.dot(s_hi, w_hi, preferred_element_type=F32)
    acc += jnp.dot(s_hi, w_mid, preferred_element_type=F32)
    acc += jnp.dot(s_mid, w_hi, preferred_element_type=F32)
    o_ref[...] = acc + b_ref[...]


def adaln_all(cond, ada_w, ada_b, tn=768):
    depth, d, n = ada_w.shape
    rows = cond.shape[0]
    return pl.pallas_call(
        _adaln_kernel,
        grid=(depth, n // tn),
        in_specs=[
            pl.BlockSpec((rows, d), lambda l, j: (0, 0)),
            pl.BlockSpec((None, d, tn), lambda l, j: (l, 0, j)),
            pl.BlockSpec((None, 1, tn), lambda l, j: (l, 0, j)),
        ],
        out_specs=pl.BlockSpec((None, rows, tn), lambda l, j: (l, 0, j)),
        out_shape=jax.ShapeDtypeStruct((depth, rows, n), F32),
        compiler_params=_params(("parallel", "parallel")),
        name="adaln",
    )(cond, ada_w, ada_b.reshape(depth, 1, n))


_PROJ_ROWS = 128


def _proj_kernel(*refs, with_dt):
    if with_dt:
        x_ref, shift_ref, scale_ref, g_ref, w_ref, wdt_ref, o_ref, odt_ref, h_scr = refs
    else:
        x_ref, shift_ref, scale_ref, g_ref, w_ref, o_ref, h_scr = refs
    tm = x_ref.shape[0]

    @pl.when(pl.program_id(1) == 0)
    def _():
        gain = g_ref[...] * (1.0 + scale_ref[...])
        shift = shift_ref[...]

        def body(r, carry):
            rows = pl.ds(pl.multiple_of(r * _PROJ_ROWS, _PROJ_ROWS), _PROJ_ROWS)
            x = x_ref[rows, :]
            ms = jnp.mean(x * x, axis=-1, keepdims=True)
            h_scr[rows, :] = (x * lax.rsqrt(ms + EPS) * gain + shift).astype(BF16)
            return carry

        lax.fori_loop(0, tm // _PROJ_ROWS, body, 0)
        if with_dt:
            odt_ref[...] = lax.dot_general(wdt_ref[...], h_scr[...], (((1,), (1,)), ((), ())),
                                           preferred_element_type=F32)

    o_ref[...] = jnp.dot(h_scr[...], w_ref[...], preferred_element_type=F32).astype(o_ref.dtype)


def project(x2, mod, layer, rows_per_seq, mod_row, g, w, wdt_t=None, *, tm, tn):
    m, d = x2.shape
    n = w.shape[1]
    tm = min(tm, rows_per_seq)
    tiles_per_seq = rows_per_seq // tm
    if mod_row is None:
        row = lambda i: i // tiles_per_seq
    else:
        row = lambda i: mod_row
    with_dt = wdt_t is not None
    in_specs = [
        pl.BlockSpec((tm, d), lambda i, j: (i, 0)),
        pl.BlockSpec((None, None, 1, d), lambda i, j: (layer, row(i), 0, 0)),
        pl.BlockSpec((None, None, 1, d), lambda i, j: (layer, row(i), 0, 1)),
        pl.BlockSpec((1, d), lambda i, j: (0, 0)),
        pl.BlockSpec((d, tn), lambda i, j: (0, j)),
    ]
    out_specs = [pl.BlockSpec((tm, tn), lambda i, j: (i, j))]
    out_shape = [jax.ShapeDtypeStruct((m, n), BF16)]
    args = [x2, mod, mod, g, w]
    if with_dt:
        in_specs.append(pl.BlockSpec((wdt_t.shape[0], d), lambda i, j: (0, 0)))
        out_specs.append(pl.BlockSpec((wdt_t.shape[0], tm), lambda i, j: (0, i)))
        out_shape.append(jax.ShapeDtypeStruct((wdt_t.shape[0], m), F32))
        args.append(wdt_t)
    outs = pl.pallas_call(
        functools.partial(_proj_kernel, with_dt=with_dt),
        grid=(m // tm, n // tn),
        in_specs=in_specs,
        out_specs=out_specs,
        out_shape=out_shape,
        scratch_shapes=[pltpu.VMEM((tm, d), BF16)],
        compiler_params=_params(("parallel", "arbitrary")),
        name="proj_dt" if with_dt else "proj",
    )(*args)
    return outs if with_dt else outs[0]


def _bias_kernel(rpb_ref, o_ref):
    h = pl.program_id(0)
    qc = lax.broadcasted_iota(jnp.int32, (GRID_W, GRID_W), 0)
    kc = lax.broadcasted_iota(jnp.int32, (GRID_W, GRID_W), 1)
    start = jnp.clip(qc - NA_KW // 2, 0, GRID_W - NA_KW)
    in_win = (kc >= start) & (kc < start + NA_KW)
    dc = jnp.clip(kc - qc, -(NA_KW - 1), NA_KW - 1) + NA_KW - 1
    for dr in range(2 * NA_KH - 1):
        acc = jnp.zeros((GRID_W, GRID_W), F32)
        for j in range(2 * NA_KW - 1):
            acc = jnp.where(dc == j, rpb_ref[h, dr * (2 * NA_KW - 1) + j], acc)
        o_ref[dr] = jnp.where(in_win, acc, MASKED)


def expand_bias(rpb):
    nh = rpb.shape[0]
    ndr, ndc = 2 * NA_KH - 1, 2 * NA_KW - 1
    return pl.pallas_call(
        _bias_kernel,
        grid=(nh,),
        in_specs=[pl.BlockSpec(memory_space=pltpu.SMEM)],
        out_specs=pl.BlockSpec((None, ndr, GRID_W, GRID_W), lambda h: (h, 0, 0, 0)),
        out_shape=jax.ShapeDtypeStruct((nh, ndr, GRID_W, GRID_W), F32),
        compiler_params=_params(("arbitrary",)),
        name="rpb_expand",
    )(rpb.reshape(nh, ndr * ndc))


def _head_norm(x, g):
    xf = x.astype(F32)
    return xf * lax.rsqrt(jnp.mean(xf * xf, axis=-1, keepdims=True) + EPS) * g


def _window_rows(case, qr, n_rows):
    if case == 1:
        lo, dr0 = qr, NA_KH - 1 - NA_KH // 2
    elif case == 0:
        lo = max(qr - NA_KH // 2, 0)
        dr0 = lo - qr + NA_KH - 1
    else:
        r = n_rows - NA_QROWS + qr
        rs = min(r - NA_KH // 2, n_rows - NA_KH)
        lo = rs - (n_rows - NA_KROWS)
        dr0 = rs - r + NA_KH - 1
    return lo, dr0


def _na_kernel(q_ref, gate_ref, k_ref, v_ref, kc_ref, vc_ref, cmat_ref, qg_ref, kg_ref, o_ref,
               kn_scr, kcn_scr, v1_scr, vc1_scr, bias_scr, s0_scr, s1_scr, p0_scr, p1_scr, *, n_rows):
    seq = k_ref.shape[0]
    nctx = kc_ref.shape[0]
    dh = NA_HEAD_DIM
    nq = NA_QROWS * GRID_W
    nk = NA_KROWS * GRID_W
    nb = n_rows // NA_QROWS
    kg = kg_ref[...]

    @pl.when(pl.program_id(1) == 0)
    def _():
        masked = jnp.full((GRID_W, GRID_W), MASKED, F32)
        for case in range(3):
            for qr in range(NA_QROWS):
                lo, dr0 = _window_rows(case, qr, n_rows)
                strip = [cmat_ref[dr0 + kr - lo] if lo <= kr < lo + NA_KH else masked
                         for kr in range(NA_KROWS)]
                bias_scr[case, qr * GRID_W:(qr + 1) * GRID_W, :] = jnp.concatenate(strip, axis=1)

    def prep(c, carry):
        rows = pl.ds(pl.multiple_of(c * 512, 512), 512)
        kn_scr[rows, :] = _head_norm(k_ref[rows, :], kg).astype(BF16)
        v1_scr[rows, 0:dh] = v_ref[rows, :]
        v1_scr[rows, dh:2 * dh] = jnp.ones((512, dh), BF16)
        return carry

    lax.fori_loop(0, seq // 512, prep, 0)
    kcn_scr[...] = _head_norm(kc_ref[...], kg).astype(BF16)
    vc1_scr[:, 0:dh] = vc_ref[...]
    vc1_scr[:, dh:2 * dh] = jnp.ones((nctx, dh), BF16)

    scale = NA_HEAD_DIM ** -0.5
    nt = (((1,), (1,)), ((), ()))

    def block_rows(i):
        qrows = pl.ds(pl.multiple_of(i * nq, nq), nq)
        kstart = jnp.clip(i * NA_QROWS - NA_KH // 2, 0, n_rows - NA_KROWS)
        krows = pl.ds(pl.multiple_of(kstart * GRID_W, GRID_W), nk)
        return qrows, krows

    def scores(i, s_scr):
        i = jnp.minimum(i, nb - 1)
        qrows, krows = block_rows(i)
        qn = (_head_norm(q_ref[qrows, :], qg_ref[...]) * scale).astype(BF16)
        case = jnp.where(i == 0, 0, jnp.where(i == nb - 1, 2, 1))
        s_scr[:, 0:nctx] = lax.dot_general(qn, kcn_scr[...], nt, preferred_element_type=F32)
        s_scr[:, nctx:nctx + nk] = (lax.dot_general(qn, kn_scr[krows, :], nt, preferred_element_type=F32)
                                    + bias_scr[case])

    def softmax(s_scr, p_scr):
        for g in range(nq // _NA_SOFTMAX_ROWS):
            rows = slice(g * _NA_SOFTMAX_ROWS, (g + 1) * _NA_SOFTMAX_ROWS)
            s = s_scr[rows, :]
            p_scr[rows, :] = jnp.exp(s - jnp.max(s, axis=-1, keepdims=True)).astype(BF16)

    def values(i, p_scr):
        qrows, krows = block_rows(i)
        o = jnp.dot(p_scr[:, 0:nctx], vc1_scr[...], preferred_element_type=F32)
        o += jnp.dot(p_scr[:, nctx:nctx + nk], v1_scr[krows, :], preferred_element_type=F32)
        gated = o[:, 0:dh] * (1.0 / o[:, dh:2 * dh]) * _silu(gate_ref[qrows, :].astype(F32))
        o_ref[qrows, :] = gated.astype(o_ref.dtype)

    scores(0, s0_scr)
    softmax(s0_scr, p0_scr)
    scores(1, s1_scr)

    def pair(jj, carry):
        j = 2 * jj
        values(j, p0_scr)
        softmax(s1_scr, p1_scr)
        scores(j + 2, s0_scr)
        values(j + 1, p1_scr)
        softmax(s0_scr, p0_scr)
        scores(j + 3, s1_scr)
        return carry

    lax.fori_loop(0, nb // 2, pair, 0)


def neighbourhood_attention(p, pc, cmat, qg, kg, cols):
    bsz, seq, _ = p.shape
    nctx = pc.shape[1]
    n_rows = seq // GRID_W
    nb = n_rows // NA_QROWS
    assert n_rows >= NA_KROWS and n_rows % NA_QROWS == 0 and nb % 2 == 0 and nctx % V7X_LANES == 0
    nq = NA_QROWS * GRID_W
    nk = NA_KROWS * GRID_W
    dh = NA_HEAD_DIM
    cq, cg, ck, cv = (cols[k] // dh for k in ("q", "gate", "k", "v"))
    slab = lambda c: pl.BlockSpec((None, seq, dh), lambda h, b: (b, 0, c + h))
    cslab = lambda c: pl.BlockSpec((None, nctx, dh), lambda h, b: (b, 0, c + h))
    return pl.pallas_call(
        functools.partial(_na_kernel, n_rows=n_rows),
        grid=(NA_HEADS, bsz),
        in_specs=[
            slab(cq), slab(cg), slab(ck), slab(cv), cslab(ck), cslab(cv),
            pl.BlockSpec((None, 2 * NA_KH - 1, GRID_W, GRID_W), lambda h, b: (h, 0, 0, 0)),
            pl.BlockSpec((1, dh), lambda h, b: (0, 0)),
            pl.BlockSpec((1, dh), lambda h, b: (0, 0)),
        ],
        out_specs=pl.BlockSpec((None, seq, dh), lambda h, b: (b, 0, h)),
        out_shape=jax.ShapeDtypeStruct((bsz, seq, NA_HEADS * dh), BF16),
        scratch_shapes=[
            pltpu.VMEM((seq, dh), BF16),
            pltpu.VMEM((nctx, dh), BF16),
            pltpu.VMEM((seq, 2 * dh), BF16),
            pltpu.VMEM((nctx, 2 * dh), BF16),
            pltpu.VMEM((3, nq, nk), F32),
            pltpu.VMEM((nq, nctx + nk), F32),
            pltpu.VMEM((nq, nctx + nk), F32),
            pltpu.VMEM((nq, nctx + nk), BF16),
            pltpu.VMEM((nq, nctx + nk), BF16),
        ],
        compiler_params=_params(("arbitrary", "arbitrary")),
        name="nbr_attention",
    )(p, p, p, p, pc, pc, cmat, qg, kg)


def _ctx_attn_kernel(q_ref, gate_ref, k_ref, v_ref, qg_ref, kg_ref, o_ref):
    scale = NA_HEAD_DIM ** -0.5
    qn = (_head_norm(q_ref[...], qg_ref[...]) * scale).astype(BF16)
    kn = _head_norm(k_ref[...], kg_ref[...]).astype(BF16)
    s = lax.dot_general(qn, kn, (((1,), (1,)), ((), ())), preferred_element_type=F32)
    m = jnp.max(s, axis=-1, keepdims=True)
    pr = jnp.exp(s - m)
    l = jnp.sum(pr, axis=-1, keepdims=True)
    o = jnp.dot(pr.astype(BF16), v_ref[...], preferred_element_type=F32)
    o_ref[...] = (o * (1.0 / l) * _silu(gate_ref[...].astype(F32))).astype(o_ref.dtype)


def context_attention(pc, qg, kg, cols):
    bsz, nctx, _ = pc.shape
    dh = NA_HEAD_DIM
    cq, cg, ck, cv = (cols[k] // dh for k in ("q", "gate", "k", "v"))
    blk = lambda c: pl.BlockSpec((None, nctx, dh), lambda b, h: (b, 0, c + h))
    return pl.pallas_call(
        _ctx_attn_kernel,
        grid=(bsz, NA_HEADS),
        in_specs=[blk(cq), blk(cg), blk(ck), blk(cv),
                  pl.BlockSpec((1, dh), lambda b, h: (0, 0)),
                  pl.BlockSpec((1, dh), lambda b, h: (0, 0))],
        out_specs=pl.BlockSpec((None, nctx, dh), lambda b, h: (b, 0, h)),
        out_shape=jax.ShapeDtypeStruct((bsz, nctx, NA_HEADS * dh), BF16),
        compiler_params=_params(("parallel", "parallel")),
        name="ctx_attention",
    )(pc, pc, pc, pc, qg, kg)


_CONV_HALO = 16


def _conv_silu_chunk(src_ref, c, nc, w_ref, b_ref, stage_ref):
    t = SSD_CHUNK
    hl = _CONV_HALO
    length = src_ref.shape[0]
    start = pl.multiple_of(c * t, t)
    prev = src_ref[pl.ds(pl.multiple_of(jnp.maximum(start - hl, 0), hl), hl), :].astype(F32)
    nxt = src_ref[pl.ds(pl.multiple_of(jnp.minimum(start + t, length - hl), hl), hl), :].astype(F32)
    stage_ref[0:hl, :] = jnp.where(c > 0, prev, 0.0)
    stage_ref[hl:hl + t, :] = src_ref[pl.ds(start, t), :].astype(F32)
    stage_ref[hl + t:hl + t + hl, :] = jnp.where(c < nc - 1, nxt, 0.0)
    k = w_ref.shape[0]
    acc = jnp.zeros((t, src_ref.shape[1]), F32) + b_ref[...]
    for j in range(k):
        off = hl - k // 2 + j
        acc += stage_ref[off:off + t, :] * w_ref[j:j + 1, :]
    return _silu(acc)


def _ssd_kernel(x_ref, bm_ref, cm_ref, z_ref, dtt_ref, wx_ref, bx_ref, wb_ref, bb_ref, wc_ref, bc_ref,
                dtb_ref, a_ref, dskip_ref, ng_ref, h0_ref, y_ref, hfin_ref,
                xs_scr, bmc_scr, cmc_scr, yacc_scr, stx_scr, stb_scr):
    t = SSD_CHUNK
    length = x_ref.shape[0]
    nc = length // t
    hp = SSD_HPG * SSD_HEAD_DIM

    def conv_chunk(c, side):
        rows = pl.ds(pl.multiple_of(c * t, t), t)
        xs_scr[rows, :] = _conv_silu_chunk(x_ref, c, nc, wx_ref, bx_ref, stx_scr.at[side]).astype(BF16)
        bmc_scr[rows, :] = _conv_silu_chunk(bm_ref, c, nc, wb_ref, bb_ref, stb_scr.at[2 * side]).astype(BF16)
        cmc_scr[rows, :] = _conv_silu_chunk(cm_ref, c, nc, wc_ref, bc_ref,
                                            stb_scr.at[2 * side + 1]).astype(BF16)

    conv_chunk(0, 0)
    conv_chunk(nc - 1, 1)
    hfin_ref[...] = h0_ref[...]

    ii = lax.broadcasted_iota(jnp.int32, (t, t), 0)
    jj = lax.broadcasted_iota(jnp.int32, (t, t), 1)
    lower = ii >= jj
    upper = ii <= jj
    indicator = lambda cond: jnp.where(cond, 1.0, 0.0).astype(BF16)
    lower_b = indicator(lower)
    upper_b = indicator(upper)
    er = lax.broadcasted_iota(jnp.int32, (2 * SSD_HPG, hp), 0)
    el = lax.broadcasted_iota(jnp.int32, (2 * SSD_HPG, hp), 1) // SSD_HEAD_DIM
    expand = [indicator(er == el + d * SSD_HPG) for d in range(2)]
    lane_head = lax.broadcasted_iota(jnp.int32, (t, hp), 1) // SSD_HEAD_DIM
    head_lanes = [indicator(lane_head == r) for r in range(SSD_HPG)]
    nt = (((1,), (1,)), ((), ()))
    tn = (((0,), (0,)), ((), ()))

    def chunk(c, d):
        rows = pl.ds(pl.multiple_of(c * t, t), t)
        dt8 = _softplus(dtt_ref[:, rows] + dtb_ref[...])
        dta8 = dt8 * a_ref[...]
        tri_row, mask = (upper_b, lower) if d == 0 else (lower_b, upper)
        hi = dta8.astype(BF16).astype(F32)
        rest = dta8 - hi
        mid = rest.astype(BF16).astype(F32)
        terms = jnp.concatenate([hi, mid, rest - mid], axis=0).astype(BF16)
        cs3 = jnp.dot(terms, tri_row, preferred_element_type=F32)
        cs_row = cs3[0:8] + cs3[8:16] + cs3[16:24]
        cs_col = cs_row.T
        dt_col = dt8.T
        edge = t - 1 if d == 0 else 0
        to_end_col = jnp.exp(cs_col[edge:edge + 1, :] - cs_col) * dt_col
        ecs = jnp.dot(jnp.exp(cs_col).astype(BF16), expand[d], preferred_element_type=F32)
        to_end = jnp.dot(to_end_col.astype(BF16), expand[d], preferred_element_type=F32)

        xs = xs_scr[rows, :]
        bmc = bmc_scr[rows, :]
        cmc = cmc_scr[rows, :]
        cb = lax.dot_general(cmc, bmc, nt, preferred_element_type=F32)
        ws, xbd = [], []
        for r in range(SSD_HPG):
            hrow = d * SSD_HPG + r
            seg = cs_col[:, hrow:hrow + 1] - cs_row[hrow:hrow + 1, :]
            w = cb * jnp.exp(jnp.where(mask, seg, MASKED)) * dt8[hrow:hrow + 1, :]
            ws.append(w.astype(BF16))
            xbd.append(xs * head_lanes[r])
        y = jnp.dot(jnp.concatenate(ws, axis=1), jnp.concatenate(xbd, axis=0), preferred_element_type=F32)
        h = hfin_ref[d]
        y += jnp.dot(cmc, h.astype(BF16), preferred_element_type=F32) * ecs
        xw = (xs.astype(F32) * to_end).astype(BF16)
        hfin_ref[d] = h * ecs[edge:edge + 1, :] + lax.dot_general(bmc, xw, tn, preferred_element_type=F32)
        return rows, xs, y

    def first_half(k, carry):
        conv_chunk(k + 1, 0)
        conv_chunk(nc - 2 - k, 1)
        rows_f, _, y_f = chunk(k, 0)
        yacc_scr[rows_f, :] = y_f
        rows_b, _, y_b = chunk(nc - 1 - k, 1)
        yacc_scr[rows_b, :] = y_b
        return carry

    def finish(rows, xs, y):
        y = y + yacc_scr[rows, :] + dskip_ref[...] * xs.astype(F32)
        yz = y * _silu(z_ref[rows, :].astype(F32))
        yz = yz * lax.rsqrt(jnp.mean(yz * yz, axis=-1, keepdims=True) + EPS)
        y_ref[rows, :] = (yz * ng_ref[...]).astype(y_ref.dtype)

    def second_half(k, carry):
        finish(*chunk(k, 0))
        finish(*chunk(nc - 1 - k, 1))
        return carry

    lax.fori_loop(0, nc // 2, first_half, 0)
    lax.fori_loop(nc // 2, nc, second_half, 0)


def ssd_mixer(p, dt_t, conv_w, conv_b, dtb8, a8, dskip, norm_g, h0, cols):
    bsz, length, _ = p.shape
    hp = SSD_HPG * SSD_HEAD_DIM
    ns = SSD_STATE
    width = SSD_GROUPS * hp
    k = conv_w.shape[0]
    cx = cols["xbc"] // hp
    cb_ = (cols["xbc"] + width) // ns
    cc_ = (cols["xbc"] + width + SSD_GROUPS * ns) // ns
    cz = cols["z"] // hp
    st_rows = SSD_CHUNK + 2 * _CONV_HALO
    return pl.pallas_call(
        _ssd_kernel,
        grid=(bsz, SSD_GROUPS),
        in_specs=[
            pl.BlockSpec((None, length, hp), lambda b, g: (b, 0, cx + g)),
            pl.BlockSpec((None, length, ns), lambda b, g: (b, 0, cb_ + g)),
            pl.BlockSpec((None, length, ns), lambda b, g: (b, 0, cc_ + g)),
            pl.BlockSpec((None, length, hp), lambda b, g: (b, 0, cz + g)),
            pl.BlockSpec((2 * SSD_HPG, length), lambda b, g: (g, b)),
            pl.BlockSpec((k, hp), lambda b, g: (0, g)),
            pl.BlockSpec((1, hp), lambda b, g: (0, g)),
            pl.BlockSpec((k, ns), lambda b, g: (0, width // ns + g)),
            pl.BlockSpec((1, ns), lambda b, g: (0, width // ns + g)),
            pl.BlockSpec((k, ns), lambda b, g: (0, width // ns + SSD_GROUPS + g)),
            pl.BlockSpec((1, ns), lambda b, g: (0, width // ns + SSD_GROUPS + g)),
            pl.BlockSpec((None, 2 * SSD_HPG, V7X_LANES), lambda b, g: (g, 0, 0)),
            pl.BlockSpec((None, 2 * SSD_HPG, V7X_LANES), lambda b, g: (g, 0, 0)),
            pl.BlockSpec((1, hp), lambda b, g: (0, g)),
            pl.BlockSpec((1, hp), lambda b, g: (0, g)),
            pl.BlockSpec((None, None, 2, ns, hp), lambda b, g: (b, g, 0, 0, 0)),
        ],
        out_specs=[
            pl.BlockSpec((None, length, hp), lambda b, g: (b, 0, g)),
            pl.BlockSpec((None, None, 2, ns, hp), lambda b, g: (b, g, 0, 0, 0)),
        ],
        out_shape=[
            jax.ShapeDtypeStruct((bsz, length, width), BF16),
            jax.ShapeDtypeStruct((bsz, SSD_GROUPS, 2, ns, hp), F32),
        ],
        scratch_shapes=[
            pltpu.VMEM((length, hp), BF16),
            pltpu.VMEM((length, ns), BF16),
            pltpu.VMEM((length, ns), BF16),
            pltpu.VMEM((length, hp), F32),
            pltpu.VMEM((2, st_rows, hp), F32),
            pltpu.VMEM((4, st_rows, ns), F32),
        ],
        compiler_params=_params(("parallel", "parallel")),
        name="ssd_mixer",
    )(p, p, p, p, dt_t, conv_w, conv_b, conv_w, conv_b, conv_w, conv_b, dtb8, a8, dskip, norm_g, h0)


def _out_kernel(*refs, n_in):
    a_refs, w_refs = refs[:n_in], refs[n_in:2 * n_in]
    x_ref, gate_ref, o_ref = refs[2 * n_in:]
    acc = jnp.dot(a_refs[0][...], w_refs[0][...], preferred_element_type=F32)
    for a_ref, w_ref in zip(a_refs[1:], w_refs[1:]):
        acc += jnp.dot(a_ref[...], w_ref[...], preferred_element_type=F32)
    o_ref[...] = x_ref[...] + gate_ref[...] * acc


def out_project(acts, w_out, x2, mod, layer, rows_per_seq, mod_row, *, tm, tn):
    m, d = x2.shape
    kw = acts[0].shape[1]
    assert all(a.shape[1] == kw for a in acts) and w_out.shape[0] == kw * len(acts)
    tm = min(tm, rows_per_seq)
    tiles_per_seq = rows_per_seq // tm
    row = (lambda i: i // tiles_per_seq) if mod_row is None else (lambda i: mod_row)
    w_spec = lambda n: pl.BlockSpec((kw, tn), lambda i, j: (n, j))
    return pl.pallas_call(
        functools.partial(_out_kernel, n_in=len(acts)),
        grid=(m // tm, d // tn),
        in_specs=[pl.BlockSpec((tm, kw), lambda i, j: (i, 0)) for _ in acts]
        + [w_spec(n) for n in range(len(acts))]
        + [pl.BlockSpec((tm, tn), lambda i, j: (i, j)),
           pl.BlockSpec((None, None, 1, tn), lambda i, j: (layer, row(i), 0, 2 * (d // tn) + j))],
        out_specs=pl.BlockSpec((tm, tn), lambda i, j: (i, j)),
        out_shape=jax.ShapeDtypeStruct((m, d), F32),
        compiler_params=_params(("parallel", "arbitrary")),
        name="out_proj",
    )(*acts, *([w_out] * len(acts)), x2, mod)


_SC_HALO = 16


def _short_conv_kernel(x_ref, xp_ref, xn_ref, shift_ref, scale_ref, g_ref, wb_ref, wc_ref, wh_ref, wg_ref,
                       cw_ref, z_ref, h_scr, u_scr, *, tiles_per_seq):
    tm = x_ref.shape[0]
    hl = _SC_HALO

    @pl.when(pl.program_id(1) == 0)
    def _():
        gain = g_ref[...] * (1.0 + scale_ref[...])
        shift = shift_ref[...]

        def norm(x):
            ms = jnp.mean(x * x, axis=-1, keepdims=True)
            return (x * lax.rsqrt(ms + EPS) * gain + shift).astype(BF16)

        h_scr[0:hl, :] = norm(xp_ref[...])
        h_scr[hl + tm:hl + tm + hl, :] = norm(xn_ref[...])

        def body(r, carry):
            rows = pl.multiple_of(r * _PROJ_ROWS, _PROJ_ROWS)
            h_scr[pl.ds(hl + rows, _PROJ_ROWS), :] = norm(x_ref[pl.ds(rows, _PROJ_ROWS), :])
            return carry

        lax.fori_loop(0, tm // _PROJ_ROWS, body, 0)

    t_in_seq = pl.program_id(0) % tiles_per_seq
    hs = h_scr[...]
    u = (jnp.dot(hs, wc_ref[...], preferred_element_type=F32)
         * jnp.dot(hs, wh_ref[...], preferred_element_type=F32))
    u_scr[0:hl, :] = jnp.where(t_in_seq > 0, u[0:hl], 0.0)
    u_scr[hl:hl + tm, :] = u[hl:hl + tm]
    u_scr[hl + tm:hl + tm + hl, :] = jnp.where(t_in_seq < tiles_per_seq - 1, u[hl + tm:], 0.0)
    k = cw_ref.shape[0]
    acc = u_scr[hl - k // 2:hl - k // 2 + tm, :] * cw_ref[0:1, :]
    for j in range(1, k):
        off = hl - k // 2 + j
        acc += u_scr[off:off + tm, :] * cw_ref[j:j + 1, :]
    hm = h_scr[hl:hl + tm, :]
    bg = jnp.dot(hm, wb_ref[...], preferred_element_type=F32)
    gt = jnp.dot(hm, wg_ref[...], preferred_element_type=F32)
    z_ref[...] = (_silu(gt) * bg * acc).astype(z_ref.dtype)


def short_conv_mixer(x2, mod, layer, rows_per_seq, mod_row, g, w_in, conv_w, *, tm, tc):
    m, d = x2.shape
    w = w_in.shape[1] // 4
    tm = min(tm, rows_per_seq)
    tiles_per_seq = rows_per_seq // tm
    row = (lambda i: i // tiles_per_seq) if mod_row is None else (lambda i: mod_row)
    hb = tm // _SC_HALO
    last = m // _SC_HALO - 1
    nblk = w // tc
    w_spec = lambda part: pl.BlockSpec((d, tc), lambda i, j: (0, part * nblk + j))
    return pl.pallas_call(
        functools.partial(_short_conv_kernel, tiles_per_seq=tiles_per_seq),
        grid=(m // tm, nblk),
        in_specs=[
            pl.BlockSpec((tm, d), lambda i, j: (i, 0)),
            pl.BlockSpec((_SC_HALO, d), lambda i, j: (jnp.maximum(i * hb - 1, 0), 0)),
            pl.BlockSpec((_SC_HALO, d), lambda i, j: (jnp.minimum((i + 1) * hb, last), 0)),
            pl.BlockSpec((None, None, 1, d), lambda i, j: (layer, row(i), 0, 0)),
            pl.BlockSpec((None, None, 1, d), lambda i, j: (layer, row(i), 0, 1)),
            pl.BlockSpec((1, d), lambda i, j: (0, 0)),
            w_spec(0), w_spec(1), w_spec(2), w_spec(3),
            pl.BlockSpec((conv_w.shape[0], tc), lambda i, j: (0, j)),
        ],
        out_specs=pl.BlockSpec((tm, tc), lambda i, j: (i, j)),
        out_shape=jax.ShapeDtypeStruct((m, w), BF16),
        scratch_shapes=[pltpu.VMEM((tm + 2 * _SC_HALO, d), BF16),
                        pltpu.VMEM((tm + 2 * _SC_HALO, tc), F32)],
        compiler_params=_params(("parallel", "arbitrary")),
        name="short_conv",
    )(x2, x2, x2, mod, mod, g, w_in, w_in, w_in, w_in, conv_w)


def _even_columns(d_model):
    na_width = NA_HEADS * NA_HEAD_DIM
    ssd_width = SSD_GROUPS * SSD_HPG * SSD_HEAD_DIM
    cols = {"q": 0}
    cols["gate"] = cols["q"] + na_width
    cols["z"] = cols["gate"] + na_width
    cols["k"] = cols["z"] + ssd_width
    cols["v"] = cols["k"] + na_width
    cols["xbc"] = cols["v"] + na_width
    cols["dt"] = cols["xbc"] + ssd_width + 2 * SSD_GROUPS * SSD_STATE
    return cols


def _dt_weight_rows(w_in, cols):
    wdt = w_in[:, cols["dt"]:cols["dt"] + 2 * SSD_HEADS]
    wdt = wdt.reshape(-1, 2, SSD_GROUPS, SSD_HPG).transpose(2, 1, 3, 0).reshape(2 * SSD_HEADS, -1)
    pad = jnp.zeros((V7X_LANES - 2 * SSD_HEADS, wdt.shape[1]), wdt.dtype)
    return jnp.concatenate([wdt, pad], axis=0).astype(BF16)


def _per_group_rows(v):
    v = v.astype(F32).reshape(2, SSD_GROUPS, SSD_HPG).transpose(1, 0, 2).reshape(SSD_GROUPS, 2 * SSD_HPG, 1)
    return jnp.broadcast_to(v, (SSD_GROUPS, 2 * SSD_HPG, V7X_LANES))


def _forward(x, c, ctx, c_ctx, ada_w, ada_b, norm_g, na_ssd_w_in, ssd_conv_w, ssd_conv_b, ssd_a_log,
             ssd_dt_bias, ssd_d, ssd_norm_g, q_norm_g, k_norm_g, na_rpb, na_ssd_w_out, sc_w_in,
             sc_conv_w, sc_w_out, *, tm, tn, tc):
    bsz, seq, d = x.shape
    nctx = ctx.shape[1]
    depth = ada_w.shape[0]
    cols = _even_columns(d)
    hp = SSD_HPG * SSD_HEAD_DIM

    cond = jnp.concatenate([c, c_ctx[None, :], jnp.zeros((8 - bsz - 1, d), F32)], axis=0)
    mod = adaln_all(cond, ada_w, ada_b).reshape(depth, 8, 1, 3 * d)
    ctx_row = bsz

    x2 = x.reshape(bsz * seq, d)
    ctx2 = ctx.reshape(bsz * nctx, d)
    for i in range(depth):
        update_ctx = any(j % 2 == 0 for j in range(i + 1, depth))
        needs_ctx = (i % 2 == 0) or update_ctx
        g = norm_g[i].reshape(1, d)
        if i % 2 == 0:
            e = i // 2
            w_in = na_ssd_w_in[e]
            w_main = w_in[:, :cols["dt"]].astype(BF16)
            wdt_t = _dt_weight_rows(w_in, cols)
            w_out = na_ssd_w_out[e].astype(BF16)
            p, dt_t = project(x2, mod, i, seq, None, g, w_main, wdt_t, tm=tm, tn=tn)
            pc, dtc_t = project(ctx2, mod, i, nctx, ctx_row, g, w_main, wdt_t, tm=tm, tn=tn)
            p3 = p.reshape(bsz, seq, -1)
            pc3 = pc.reshape(bsz, nctx, -1)
            qg = q_norm_g[e].reshape(1, -1)
            kg = k_norm_g[e].reshape(1, -1)
            cmat = expand_bias(na_rpb[e])
            ya = neighbourhood_attention(p3, pc3, cmat, qg, kg, cols)
            dtb8 = _per_group_rows(ssd_dt_bias[e])
            a8 = _per_group_rows(-jnp.exp(ssd_a_log[e].astype(F32)))
            dskip = jnp.repeat(ssd_d[e].astype(F32), SSD_HEAD_DIM).reshape(1, -1)
            sng = ssd_norm_g[e].reshape(1, -1)
            cw = ssd_conv_w[e]
            cb = ssd_conv_b[e].reshape(1, -1)
            h_zero = jnp.zeros((bsz, SSD_GROUPS, 2, SSD_STATE, hp), F32)
            ybc, h_ctx = ssd_mixer(pc3, dtc_t, cw, cb, dtb8, a8, dskip, sng, h_zero, cols)
            yb, _ = ssd_mixer(p3, dt_t, cw, cb, dtb8, a8, dskip, sng, h_ctx, cols)
            x2_new = out_project([ya.reshape(bsz * seq, -1), yb.reshape(bsz * seq, -1)], w_out, x2,
                                 mod, i, seq, None, tm=tm, tn=tn)
            if update_ctx:
                yac = context_attention(pc3, qg, kg, cols)
                ctx2 = out_project([yac.reshape(bsz * nctx, -1), ybc.reshape(bsz * nctx, -1)], w_out,
                                   ctx2, mod, i, nctx, ctx_row, tm=tm, tn=tn)
            x2 = x2_new
        else:
            o = i // 2
            w_in = sc_w_in[o].astype(BF16)
            w_out = sc_w_out[o].astype(BF16)
            z = short_conv_mixer(x2, mod, i, seq, None, g, w_in, sc_conv_w[o], tm=tm, tc=tc)
            x2_new = out_project([z], w_out, x2, mod, i, seq, None, tm=tm, tn=tn)
            if update_ctx:
                zc = short_conv_mixer(ctx2, mod, i, nctx, ctx_row, g, w_in, sc_conv_w[o], tm=tm, tc=tc)
                ctx2 = out_project([zc], w_out, ctx2, mod, i, nctx, ctx_row, tm=tm, tn=tn)
            x2 = x2_new
    return x2.reshape(bsz, seq, d)


def kernel(x, c, ctx, c_ctx, ada_w, ada_b, norm_g, na_ssd_w_in, ssd_conv_w, ssd_conv_b, ssd_a_log,
           ssd_dt_bias, ssd_d, ssd_norm_g, q_norm_g, k_norm_g, na_rpb, na_ssd_w_out, sc_w_in,
           sc_conv_w, sc_w_out):
    return _forward(x, c, ctx, c_ctx, ada_w, ada_b, norm_g, na_ssd_w_in, ssd_conv_w, ssd_conv_b,
                    ssd_a_log, ssd_dt_bias, ssd_d, ssd_norm_g, q_norm_g, k_norm_g, na_rpb,
                    na_ssd_w_out, sc_w_in, sc_conv_w, sc_w_out, tm=1024, tn=512, tc=256)
```

```python
import functools

import jax
import jax.numpy as jnp
import numpy as np
from jax import lax
from jax.experimental import pallas as pl
from jax.experimental.pallas import tpu as pltpu

F32 = jnp.float32
BF16 = jnp.bfloat16

EPS = 1e-6
GRID_W = 64
NA_HEADS = 16
NA_HEAD_DIM = 128
NA_KH = 8
NA_KW = 16
SSD_HEAD_DIM = 64
SSD_GROUPS = 8
SSD_HPG = 4
SSD_HEADS = SSD_GROUPS * SSD_HPG
SSD_STATE = 128
SSD_CONV = 5
SSD_CHUNK = 128
SC_CONV = 3

V7X_LANES = 128
V7X_VMEM_LIMIT = 56 * 1024 * 1024

MASKED = -1e30

NA_QROWS = 8
NA_KROWS = 16
_NA_SOFTMAX_ROWS = 32


def _silu(x):
    return x * (1.0 / (1.0 + jnp.exp(-x)))


def _softplus(x):
    return jnp.maximum(x, 0.0) + jnp.log(1.0 + jnp.exp(-jnp.abs(x)))


def _split3(a):
    hi = a.astype(BF16)
    r1 = a - hi.astype(F32)
    mid = r1.astype(BF16)
    lo = (r1 - mid.astype(F32)).astype(BF16)
    return hi, mid, lo


def _params(sem, vmem=V7X_VMEM_LIMIT):
    return pltpu.CompilerParams(dimension_semantics=sem, vmem_limit_bytes=vmem)


def _adaln_kernel(cond_ref, w_ref, b_ref, o_ref):
    s = _silu(cond_ref[...])
    w = w_ref[...]
    s_hi, s_mid, _ = _split3(s)
    w_hi, w_mid, _ = _split3(w)
    acc = jnp.dot(s_hi, w_hi, preferred_element_type=F32)
    acc += jnp.dot(s_hi, w_mid, preferred_element_type=F32)
    acc += jnp.dot(s_mid, w_hi, preferred_element_type=F32)
    o_ref[...] = acc + b_ref[...]


def adaln_all(cond, ada_w, ada_b, tn=768):
    depth, d, n = ada_w.shape
    rows = cond.shape[0]
    return pl.pallas_call(
        _adaln_kernel,
        grid=(depth, n // tn),
        in_specs=[
            pl.BlockSpec((rows, d), lambda l, j: (0, 0)),
            pl.BlockSpec((None, d, tn), lambda l, j: (l, 0, j)),
            pl.BlockSpec((None, 1, tn), lambda l, j: (l, 0, j)),
        ],
        out_specs=pl.BlockSpec((None, rows, tn), lambda l, j: (l, 0, j)),
        out_shape=jax.ShapeDtypeStruct((depth, rows, n), F32),
        compiler_params=_params(("parallel", "parallel")),
        name="adaln",
    )(cond, ada_w, ada_b.reshape(depth, 1, n))


_PROJ_ROWS = 128


def _proj_kernel(*refs, with_dt):
    if with_dt:
        x_ref, shift_ref, scale_ref, g_ref, w_ref, wdt_ref, o_ref, odt_ref, h_scr = refs
    else:
        x_ref, shift_ref, scale_ref, g_ref, w_ref, o_ref, h_scr = refs
    tm = x_ref.shape[0]

    @pl.when(pl.program_id(1) == 0)
    def _():
        gain = g_ref[...] * (1.0 + scale_ref[...])
        shift = shift_ref[...]

        def body(r, carry):
            rows = pl.ds(pl.multiple_of(r * _PROJ_ROWS, _PROJ_ROWS), _PROJ_ROWS)
            x = x_ref[rows, :]
            ms = jnp.mean(x * x, axis=-1, keepdims=True)
            h_scr[rows, :] = (x * lax.rsqrt(ms + EPS) * gain + shift).astype(BF16)
            return carry

        lax.fori_loop(0, tm // _PROJ_ROWS, body, 0)
        if with_dt:
            odt_ref[...] = lax.dot_general(wdt_ref[...], h_scr[...], (((1,), (1,)), ((), ())),
                                           preferred_element_type=F32)

    o_ref[...] = jnp.dot(h_scr[...], w_ref[...], preferred_element_type=F32).astype(o_ref.dtype)


def project(x2, mod, layer, rows_per_seq, mod_row, g, w, wdt_t=None, *, tm, tn):
    m, d = x2.shape
    n = w.shape[1]
    tm = min(tm, rows_per_seq)
    tiles_per_seq = rows_per_seq // tm
    if mod_row is None:
        row = lambda i: i // tiles_per_seq
    else:
        row = lambda i: mod_row
    with_dt = wdt_t is not None
    in_specs = [
        pl.BlockSpec((tm, d), lambda i, j: (i, 0)),
        pl.BlockSpec((None, None, 1, d), lambda i, j: (layer, row(i), 0, 0)),
        pl.BlockSpec((None, None, 1, d), lambda i, j: (layer, row(i), 0, 1)),
        pl.BlockSpec((1, d), lambda i, j: (0, 0)),
        pl.BlockSpec((d, tn), lambda i, j: (0, j)),
    ]
    out_specs = [pl.BlockSpec((tm, tn), lambda i, j: (i, j))]
    out_shape = [jax.ShapeDtypeStruct((m, n), BF16)]
    args = [x2, mod, mod, g, w]
    if with_dt:
        in_specs.append(pl.BlockSpec((wdt_t.shape[0], d), lambda i, j: (0, 0)))
        out_specs.append(pl.BlockSpec((wdt_t.shape[0], tm), lambda i, j: (0, i)))
        out_shape.append(jax.ShapeDtypeStruct((wdt_t.shape[0], m), F32))
        args.append(wdt_t)
    outs = pl.pallas_call(
        functools.partial(_proj_kernel, with_dt=with_dt),
        grid=(m // tm, n // tn),
        in_specs=in_specs,
        out_specs=out_specs,
        out_shape=out_shape,
        scratch_shapes=[pltpu.VMEM((tm, d), BF16)],
        compiler_params=_params(("parallel", "arbitrary")),
        name="proj_dt" if with_dt else "proj",
    )(*args)
    return outs if with_dt else outs[0]


def _bias_kernel(rpb_ref, o_ref):
    h = pl.program_id(0)
    qc = lax.broadcasted_iota(jnp.int32, (GRID_W, GRID_W), 0)
    kc = lax.broadcasted_iota(jnp.int32, (GRID_W, GRID_W), 1)
    start = jnp.clip(qc - NA_KW // 2, 0, GRID_W - NA_KW)
    in_win = (kc >= start) & (kc < start + NA_KW)
    dc = jnp.clip(kc - qc, -(NA_KW - 1), NA_KW - 1) + NA_KW - 1
    for dr in range(2 * NA_KH - 1):
        acc = jnp.zeros((GRID_W, GRID_W), F32)
        for j in range(2 * NA_KW - 1):
            acc = jnp.where(dc == j, rpb_ref[h, dr * (2 * NA_KW - 1) + j], acc)
        o_ref[dr] = jnp.where(in_win, acc, MASKED)


def expand_bias(rpb):
    nh = rpb.shape[0]
    ndr, ndc = 2 * NA_KH - 1, 2 * NA_KW - 1
    return pl.pallas_call(
        _bias_kernel,
        grid=(nh,),
        in_specs=[pl.BlockSpec(memory_space=pltpu.SMEM)],
        out_specs=pl.BlockSpec((None, ndr, GRID_W, GRID_W), lambda h: (h, 0, 0, 0)),
        out_shape=jax.ShapeDtypeStruct((nh, ndr, GRID_W, GRID_W), F32),
        compiler_params=_params(("arbitrary",)),
        name="rpb_expand",
    )(rpb.reshape(nh, ndr * ndc))


def _head_norm(x, g):
    xf = x.astype(F32)
    return xf * lax.rsqrt(jnp.mean(xf * xf, axis=-1, keepdims=True) + EPS) * g


def _window_rows(case, qr, n_rows):
    if case == 1:
        lo, dr0 = qr, NA_KH - 1 - NA_KH // 2
    elif case == 0:
        lo = max(qr - NA_KH // 2, 0)
        dr0 = lo - qr + NA_KH - 1
    else:
        r = n_rows - NA_QROWS + qr
        rs = min(r - NA_KH // 2, n_rows - NA_KH)
        lo = rs - (n_rows - NA_KROWS)
        dr0 = rs - r + NA_KH - 1
    return lo, dr0


def _na_kernel(q_ref, gate_ref, k_ref, v_ref, kc_ref, vc_ref, cmat_ref, qg_ref, kg_ref, o_ref,
               kn_scr, kcn_scr, v1_scr, vc1_scr, bias_scr, s0_scr, s1_scr, p0_scr, p1_scr, *, n_rows):
    seq = k_ref.shape[0]
    nctx = kc_ref.shape[0]
    dh = NA_HEAD_DIM
    nq = NA_QROWS * GRID_W
    nk = NA_KROWS * GRID_W
    nb = n_rows // NA_QROWS
    kg = kg_ref[...]

    @pl.when(pl.program_id(1) == 0)
    def _():
        masked = jnp.full((GRID_W, GRID_W), MASKED, F32)
        for case in range(3):
            for qr in range(NA_QROWS):
                lo, dr0 = _window_rows(case, qr, n_rows)
                strip = [cmat_ref[dr0 + kr - lo] if lo <= kr < lo + NA_KH else masked
                         for kr in range(NA_KROWS)]
                bias_scr[case, qr * GRID_W:(qr + 1) * GRID_W, :] = jnp.concatenate(strip, axis=1)

    def prep(c, carry):
        rows = pl.ds(pl.multiple_of(c * 512, 512), 512)
        kn_scr[rows, :] = _head_norm(k_ref[rows, :], kg).astype(BF16)
        v1_scr[rows, 0:dh] = v_ref[rows, :]
        v1_scr[rows, dh:2 * dh] = jnp.ones((512, dh), BF16)
        return carry

    lax.fori_loop(0, seq // 512, prep, 0)
    kcn_scr[...] = _head_norm(kc_ref[...], kg).astype(BF16)
    vc1_scr[:, 0:dh] = vc_ref[...]
    vc1_scr[:, dh:2 * dh] = jnp.ones((nctx, dh), BF16)

    scale = NA_HEAD_DIM ** -0.5
    nt = (((1,), (1,)), ((), ()))

    def block_rows(i):
        qrows = pl.ds(pl.multiple_of(i * nq, nq), nq)
        kstart = jnp.clip(i * NA_QROWS - NA_KH // 2, 0, n_rows - NA_KROWS)
        krows = pl.ds(pl.multiple_of(kstart * GRID_W, GRID_W), nk)
        return qrows, krows

    def scores(i, s_scr):
        i = jnp.minimum(i, nb - 1)
        qrows, krows = block_rows(i)
        qn = (_head_norm(q_ref[qrows, :], qg_ref[...]) * scale).astype(BF16)
        case = jnp.where(i == 0, 0, jnp.where(i == nb - 1, 2, 1))
        s_scr[:, 0:nctx] = lax.dot_general(qn, kcn_scr[...], nt, preferred_element_type=F32)
        s_scr[:, nctx:nctx + nk] = (lax.dot_general(qn, kn_scr[krows, :], nt, preferred_element_type=F32)
                                    + bias_scr[case])

    def softmax(s_scr, p_scr):
        for g in range(nq // _NA_SOFTMAX_ROWS):
            rows = slice(g * _NA_SOFTMAX_ROWS, (g + 1) * _NA_SOFTMAX_ROWS)
            s = s_scr[rows, :]
            p_scr[rows, :] = jnp.exp(s - jnp.max(s, axis=-1, keepdims=True)).astype(BF16)

    def values(i, p_scr):
        qrows, krows = block_rows(i)
        o = jnp.dot(p_scr[:, 0:nctx], vc1_scr[...], preferred_element_type=F32)
        o += jnp.dot(p_scr[:, nctx:nctx + nk], v1_scr[krows, :], preferred_element_type=F32)
        gated = o[:, 0:dh] * (1.0 / o[:, dh:2 * dh]) * _silu(gate_ref[qrows, :].astype(F32))
        o_ref[qrows, :] = gated.astype(o_ref.dtype)

    scores(0, s0_scr)
    softmax(s0_scr, p0_scr)
    scores(1, s1_scr)

    def pair(jj, carry):
        j = 2 * jj
        values(j, p0_scr)
        softmax(s1_scr, p1_scr)
        scores(j + 2, s0_scr)
        values(j + 1, p1_scr)
        softmax(s0_scr, p0_scr)
        scores(j + 3, s1_scr)
        return carry

    lax.fori_loop(0, nb // 2, pair, 0)


def neighbourhood_attention(p, pc, cmat, qg, kg, cols):
    bsz, seq, _ = p.shape
    nctx = pc.shape[1]
    n_rows = seq // GRID_W
    nb = n_rows // NA_QROWS
    assert n_rows >= NA_KROWS and n_rows % NA_QROWS == 0 and nb % 2 == 0 and nctx % V7X_LANES == 0
    nq = NA_QROWS * GRID_W
    nk = NA_KROWS * GRID_W
    dh = NA_HEAD_DIM
    cq, cg, ck, cv = (cols[k] // dh for k in ("q", "gate", "k", "v"))
    slab = lambda c: pl.BlockSpec((None, seq, dh), lambda h, b: (b, 0, c + h))
    cslab = lambda c: pl.BlockSpec((None, nctx, dh), lambda h, b: (b, 0, c + h))
    return pl.pallas_call(
        functools.partial(_na_kernel, n_rows=n_rows),
        grid=(NA_HEADS, bsz),
        in_specs=[
            slab(cq), slab(cg), slab(ck), slab(cv), cslab(ck), cslab(cv),
            pl.BlockSpec((None, 2 * NA_KH - 1, GRID_W, GRID_W), lambda h, b: (h, 0, 0, 0)),
            pl.BlockSpec((1, dh), lambda h, b: (0, 0)),
            pl.BlockSpec((1, dh), lambda h, b: (0, 0)),
        ],
        out_specs=pl.BlockSpec((None, seq, dh), lambda h, b: (b, 0, h)),
        out_shape=jax.ShapeDtypeStruct((bsz, seq, NA_HEADS * dh), BF16),
        scratch_shapes=[
            pltpu.VMEM((seq, dh), BF16),
            pltpu.VMEM((nctx, dh), BF16),
            pltpu.VMEM((seq, 2 * dh), BF16),
            pltpu.VMEM((nctx, 2 * dh), BF16),
            pltpu.VMEM((3, nq, nk), F32),
            pltpu.VMEM((nq, nctx + nk), F32),
            pltpu.VMEM((nq, nctx + nk), F32),
            pltpu.VMEM((nq, nctx + nk), BF16),
            pltpu.VMEM((nq, nctx + nk), BF16),
        ],
        compiler_params=_params(("arbitrary", "arbitrary")),
        name="nbr_attention",
    )(p, p, p, p, pc, pc, cmat, qg, kg)


def _ctx_attn_kernel(q_ref, gate_ref, k_ref, v_ref, qg_ref, kg_ref, o_ref):
    scale = NA_HEAD_DIM ** -0.5
    qn = (_head_norm(q_ref[...], qg_ref[...]) * scale).astype(BF16)
    kn = _head_norm(k_ref[...], kg_ref[...]).astype(BF16)
    s = lax.dot_general(qn, kn, (((1,), (1,)), ((), ())), preferred_element_type=F32)
    m = jnp.max(s, axis=-1, keepdims=True)
    pr = jnp.exp(s - m)
    l = jnp.sum(pr, axis=-1, keepdims=True)
    o = jnp.dot(pr.astype(BF16), v_ref[...], preferred_element_type=F32)
    o_ref[...] = (o * (1.0 / l) * _silu(gate_ref[...].astype(F32))).astype(o_ref.dtype)


def context_attention(pc, qg, kg, cols):
    bsz, nctx, _ = pc.shape
    dh = NA_HEAD_DIM
    cq, cg, ck, cv = (cols[k] // dh for k in ("q", "gate", "k", "v"))
    blk = lambda c: pl.BlockSpec((None, nctx, dh), lambda b, h: (b, 0, c + h))
    return pl.pallas_call(
        _ctx_attn_kernel,
        grid=(bsz, NA_HEADS),
        in_specs=[blk(cq), blk(cg), blk(ck), blk(cv),
                  pl.BlockSpec((1, dh), lambda b, h: (0, 0)),
                  pl.BlockSpec((1, dh), lambda b, h: (0, 0))],
        out_specs=pl.BlockSpec((None, nctx, dh), lambda b, h: (b, 0, h)),
        out_shape=jax.ShapeDtypeStruct((bsz, nctx, NA_HEADS * dh), BF16),
        compiler_params=_params(("parallel", "parallel")),
        name="ctx_attention",
    )(pc, pc, pc, pc, qg, kg)


_CONV_HALO = 16


def _conv_silu_chunk(src_ref, c, nc, w_ref, b_ref, stage_ref):
    t = SSD_CHUNK
    hl = _CONV_HALO
    length = src_ref.shape[0]
    start = pl.multiple_of(c * t, t)
    prev = src_ref[pl.ds(pl.multiple_of(jnp.maximum(start - hl, 0), hl), hl), :].astype(F32)
    nxt = src_ref[pl.ds(pl.multiple_of(jnp.minimum(start + t, length - hl), hl), hl), :].astype(F32)
    stage_ref[0:hl, :] = jnp.where(c > 0, prev, 0.0)
    stage_ref[hl:hl + t, :] = src_ref[pl.ds(start, t), :].astype(F32)
    stage_ref[hl + t:hl + t + hl, :] = jnp.where(c < nc - 1, nxt, 0.0)
    k = w_ref.shape[0]
    acc = jnp.zeros((t, src_ref.shape[1]), F32) + b_ref[...]
    for j in range(k):
        off = hl - k // 2 + j
        acc += stage_ref[off:off + t, :] * w_ref[j:j + 1, :]
    return _silu(acc)


def _ssd_kernel(x_ref, bm_ref, cm_ref, z_ref, dtt_ref, wx_ref, bx_ref, wb_ref, bb_ref, wc_ref, bc_ref,
                dtb_ref, a_ref, dskip_ref, ng_ref, h0_ref, y_ref, hfin_ref,
                xs_scr, bmc_scr, cmc_scr, yacc_scr, stx_scr, stb_scr, rows_scr, col_scr, fac_scr):
    t = SSD_CHUNK
    length = x_ref.shape[0]
    nc = length // t
    hp = SSD_HPG * SSD_HEAD_DIM

    def conv_chunk(c, side):
        rows = pl.ds(pl.multiple_of(c * t, t), t)
        xs_scr[rows, :] = _conv_silu_chunk(x_ref, c, nc, wx_ref, bx_ref, stx_scr.at[side]).astype(BF16)
        bmc_scr[rows, :] = _conv_silu_chunk(bm_ref, c, nc, wb_ref, bb_ref, stb_scr.at[2 * side]).astype(BF16)
        cmc_scr[rows, :] = _conv_silu_chunk(cm_ref, c, nc, wc_ref, bc_ref,
                                            stb_scr.at[2 * side + 1]).astype(BF16)

    conv_chunk(0, 0)
    conv_chunk(nc - 1, 1)
    hfin_ref[...] = h0_ref[...]

    ii = lax.broadcasted_iota(jnp.int32, (t, t), 0)
    jj = lax.broadcasted_iota(jnp.int32, (t, t), 1)
    lower = ii >= jj
    diag = ii == jj
    indicator = lambda cond: jnp.where(cond, 1.0, 0.0).astype(BF16)
    tri2_b = jnp.concatenate([indicator(ii <= jj), indicator(lower)], axis=1)
    fwd_rows = lax.broadcasted_iota(jnp.int32, (2 * SSD_HPG, t), 0) < SSD_HPG
    er = lax.broadcasted_iota(jnp.int32, (2 * SSD_HPG, hp), 0)
    el = lax.broadcasted_iota(jnp.int32, (2 * SSD_HPG, hp), 1) // SSD_HEAD_DIM
    expand = [indicator(er == el + d * SSD_HPG) for d in range(2)]
    lane_head = lax.broadcasted_iota(jnp.int32, (t, hp), 1) // SSD_HEAD_DIM
    head_lanes = [indicator(lane_head == r) for r in range(SSD_HPG)]
    nt = (((1,), (1,)), ((), ()))
    tn = (((0,), (0,)), ((), ()))

    def chunk_rows(c):
        return pl.ds(pl.multiple_of(c * t, t), t)

    def decay_terms(c):
        dt8 = _softplus(dtt_ref[:, chunk_rows(c)] + dtb_ref[...])
        dta8 = dt8 * a_ref[...]
        hi = dta8.astype(BF16).astype(F32)
        rest = dta8 - hi
        mid = rest.astype(BF16).astype(F32)
        terms = jnp.concatenate([hi, mid, rest - mid], axis=0).astype(BF16)
        cs3 = jnp.dot(terms, tri2_b, preferred_element_type=F32)
        cs2 = cs3[0:8] + cs3[8:16] + cs3[16:24]
        cs_row = jnp.where(fwd_rows, cs2[:, 0:t], cs2[:, t:2 * t])
        return dt8, cs_row, dt8.T, cs_row.T

    def factors(d, dt_col, cs_col):
        edge = t - 1 if d == 0 else 0
        to_end_col = jnp.exp(jnp.minimum(cs_col[edge:edge + 1, :] - cs_col, 0.0)) * dt_col
        ecs = jnp.dot(jnp.exp(cs_col).astype(BF16), expand[d], preferred_element_type=F32)
        to_end = jnp.dot(to_end_col.astype(BF16), expand[d], preferred_element_type=F32)
        return ecs, to_end

    def prepare(k, buf):
        dt8, cs_row, dt_col, cs_col = decay_terms(jnp.minimum(k, nc - 1))
        rows_scr[buf, 0:8, :] = dt8
        rows_scr[buf, 8:16, :] = cs_row
        col_scr[buf] = cs_col
        ecs, to_end = factors(0, dt_col, cs_col)
        fac_scr[buf, 0] = ecs
        fac_scr[buf, 1] = to_end
        _, _, dt_col_b, cs_col_b = decay_terms(jnp.maximum(nc - 1 - k, 0))
        ecs_b, to_end_b = factors(1, dt_col_b, cs_col_b)
        fac_scr[buf, 2] = ecs_b
        fac_scr[buf, 3] = to_end_b

    def intra(c, buf):
        dt8 = rows_scr[buf, 0:8, :]
        cs_row = rows_scr[buf, 8:16, :]
        cs_col = col_scr[buf]
        rows = chunk_rows(c)
        xs = xs_scr[rows, :]
        cb = lax.dot_general(cmc_scr[rows, :], bmc_scr[rows, :], nt, preferred_element_type=F32)
        cb_diag = jnp.where(diag, cb, 0.0)
        ws, xbd = [], []
        for r in range(SSD_HPG):
            f, b = r, SSD_HPG + r
            seg = jnp.where(lower, cs_col[:, f:f + 1] - cs_row[f:f + 1, :],
                            cs_col[:, b:b + 1] - cs_row[b:b + 1, :])
            dt_sel = jnp.where(lower, dt8[f:f + 1, :], dt8[b:b + 1, :])
            w = cb * (jnp.exp(seg) * dt_sel) + cb_diag * dt8[b:b + 1, :]
            ws.append(w.astype(BF16))
            xbd.append(xs * head_lanes[r])
        return jnp.dot(jnp.concatenate(ws, axis=1), jnp.concatenate(xbd, axis=0), preferred_element_type=F32)

    def state(c, d, buf):
        rows = chunk_rows(c)
        edge = t - 1 if d == 0 else 0
        ecs = fac_scr[buf, 2 * d]
        to_end = fac_scr[buf, 2 * d + 1]
        xs = xs_scr[rows, :]
        h = hfin_ref[d]
        y = jnp.dot(cmc_scr[rows, :], h.astype(BF16), preferred_element_type=F32) * ecs
        xw = (xs.astype(F32) * to_end).astype(BF16)
        hfin_ref[d] = h * ecs[edge:edge + 1, :] + lax.dot_general(bmc_scr[rows, :], xw, tn,
                                                                  preferred_element_type=F32)
        return y

    def finish(c, y):
        rows = chunk_rows(c)
        y = y + yacc_scr[rows, :] + dskip_ref[...] * xs_scr[rows, :].astype(F32)
        yz = y * _silu(z_ref[rows, :].astype(F32))
        yz = yz * lax.rsqrt(jnp.mean(yz * yz, axis=-1, keepdims=True) + EPS)
        y_ref[rows, :] = (yz * ng_ref[...]).astype(y_ref.dtype)

    def first_half(k, buf):
        yacc_scr[chunk_rows(k), :] = intra(k, buf) + state(k, 0, buf)
        yacc_scr[chunk_rows(nc - 1 - k), :] = state(nc - 1 - k, 1, buf)
        prepare(k + 1, 1 - buf)
        conv_chunk(k + 1, 0)
        conv_chunk(nc - 2 - k, 1)

    def second_half(k, buf):
        finish(k, intra(k, buf) + state(k, 0, buf))
        finish(nc - 1 - k, state(nc - 1 - k, 1, buf))
        prepare(k + 1, 1 - buf)

    def walk(body, start, stop):
        if start % 2 == 0 and (stop - start) % 2 == 0 and stop - start > 2:
            def pair(kk, carry):
                body(2 * kk, 0)
                body(2 * kk + 1, 1)
                return carry

            lax.fori_loop(start // 2, stop // 2, pair, 0)
        else:
            for k in range(start, stop):
                body(jnp.int32(k), k % 2)

    prepare(0, 0)
    walk(first_half, 0, nc // 2)
    walk(second_half, nc // 2, nc)


def ssd_mixer(p, dt_t, conv_w, conv_b, dtb8, a8, dskip, norm_g, h0, cols):
    bsz, length, _ = p.shape
    hp = SSD_HPG * SSD_HEAD_DIM
    ns = SSD_STATE
    width = SSD_GROUPS * hp
    k = conv_w.shape[0]
    cx = cols["xbc"] // hp
    cb_ = (cols["xbc"] + width) // ns
    cc_ = (cols["xbc"] + width + SSD_GROUPS * ns) // ns
    cz = cols["z"] // hp
    st_rows = SSD_CHUNK + 2 * _CONV_HALO
    return pl.pallas_call(
        _ssd_kernel,
        grid=(bsz, SSD_GROUPS),
        in_specs=[
            pl.BlockSpec((None, length, hp), lambda b, g: (b, 0, cx + g)),
            pl.BlockSpec((None, length, ns), lambda b, g: (b, 0, cb_ + g)),
            pl.BlockSpec((None, length, ns), lambda b, g: (b, 0, cc_ + g)),
            pl.BlockSpec((None, length, hp), lambda b, g: (b, 0, cz + g)),
            pl.BlockSpec((2 * SSD_HPG, length), lambda b, g: (g, b)),
            pl.BlockSpec((k, hp), lambda b, g: (0, g)),
            pl.BlockSpec((1, hp), lambda b, g: (0, g)),
            pl.BlockSpec((k, ns), lambda b, g: (0, width // ns + g)),
            pl.BlockSpec((1, ns), lambda b, g: (0, width // ns + g)),
            pl.BlockSpec((k, ns), lambda b, g: (0, width // ns + SSD_GROUPS + g)),
            pl.BlockSpec((1, ns), lambda b, g: (0, width // ns + SSD_GROUPS + g)),
            pl.BlockSpec((None, 2 * SSD_HPG, V7X_LANES), lambda b, g: (g, 0, 0)),
            pl.BlockSpec((None, 2 * SSD_HPG, V7X_LANES), lambda b, g: (g, 0, 0)),
            pl.BlockSpec((1, hp), lambda b, g: (0, g)),
            pl.BlockSpec((1, hp), lambda b, g: (0, g)),
            pl.BlockSpec((None, None, 2, ns, hp), lambda b, g: (b, g, 0, 0, 0)),
        ],
        out_specs=[
            pl.BlockSpec((None, length, hp), lambda b, g: (b, 0, g)),
            pl.BlockSpec((None, None, 2, ns, hp), lambda b, g: (b, g, 0, 0, 0)),
        ],
        out_shape=[
            jax.ShapeDtypeStruct((bsz, length, width), BF16),
            jax.ShapeDtypeStruct((bsz, SSD_GROUPS, 2, ns, hp), F32),
        ],
        scratch_shapes=[
            pltpu.VMEM((length, hp), BF16),
            pltpu.VMEM((length, ns), BF16),
            pltpu.VMEM((length, ns), BF16),
            pltpu.VMEM((length, hp), F32),
            pltpu.VMEM((2, st_rows, hp), F32),
            pltpu.VMEM((4, st_rows, ns), F32),
            pltpu.VMEM((2, 4 * SSD_HPG, SSD_CHUNK), F32),
            pltpu.VMEM((2, SSD_CHUNK, 2 * SSD_HPG), F32),
            pltpu.VMEM((2, 4, SSD_CHUNK, hp), F32),
        ],
        compiler_params=_params(("parallel", "parallel")),
        name="ssd_mixer",
    )(p, p, p, p, dt_t, conv_w, conv_b, conv_w, conv_b, conv_w, conv_b, dtb8, a8, dskip, norm_g, h0)


def _out_kernel(*refs, n_in):
    a_refs, w_refs = refs[:n_in], refs[n_in:2 * n_in]
    x_ref, gate_ref, o_ref = refs[2 * n_in:]
    acc = jnp.dot(a_refs[0][...], w_refs[0][...], preferred_element_type=F32)
    for a_ref, w_ref in zip(a_refs[1:], w_refs[1:]):
        acc += jnp.dot(a_ref[...], w_ref[...], preferred_element_type=F32)
    o_ref[...] = x_ref[...] + gate_ref[...] * acc


def out_project(acts, w_out, x2, mod, layer, rows_per_seq, mod_row, *, tm, tn):
    m, d = x2.shape
    kw = acts[0].shape[1]
    assert all(a.shape[1] == kw for a in acts) and w_out.shape[0] == kw * len(acts)
    tm = min(tm, rows_per_seq)
    tiles_per_seq = rows_per_seq // tm
    row = (lambda i: i // tiles_per_seq) if mod_row is None else (lambda i: mod_row)
    w_spec = lambda n: pl.BlockSpec((kw, tn), lambda i, j: (n, j))
    return pl.pallas_call(
        functools.partial(_out_kernel, n_in=len(acts)),
        grid=(m // tm, d // tn),
        in_specs=[pl.BlockSpec((tm, kw), lambda i, j: (i, 0)) for _ in acts]
        + [w_spec(n) for n in range(len(acts))]
        + [pl.BlockSpec((tm, tn), lambda i, j: (i, j)),
           pl.BlockSpec((None, None, 1, tn), lambda i, j: (layer, row(i), 0, 2 * (d // tn) + j))],
        out_specs=pl.BlockSpec((tm, tn), lambda i, j: (i, j)),
        out_shape=jax.ShapeDtypeStruct((m, d), F32),
        compiler_params=_params(("parallel", "arbitrary")),
        name="out_proj",
    )(*acts, *([w_out] * len(acts)), x2, mod)


_SC_HALO = 16


def _short_conv_kernel(x_ref, xp_ref, xn_ref, shift_ref, scale_ref, g_ref, wb_ref, wc_ref, wh_ref, wg_ref,
                       cw_ref, z_ref, h_scr, u_scr, *, tiles_per_seq):
    tm = x_ref.shape[0]
    hl = _SC_HALO

    @pl.when(pl.program_id(1) == 0)
    def _():
        gain = g_ref[...] * (1.0 + scale_ref[...])
        shift = shift_ref[...]

        def norm(x):
            ms = jnp.mean(x * x, axis=-1, keepdims=True)
            return (x * lax.rsqrt(ms + EPS) * gain + shift).astype(BF16)

        h_scr[0:hl, :] = norm(xp_ref[...])
        h_scr[hl + tm:hl + tm + hl, :] = norm(xn_ref[...])

        def body(r, carry):
            rows = pl.multiple_of(r * _PROJ_ROWS, _PROJ_ROWS)
            h_scr[pl.ds(hl + rows, _PROJ_ROWS), :] = norm(x_ref[pl.ds(rows, _PROJ_ROWS), :])
            return carry

        lax.fori_loop(0, tm // _PROJ_ROWS, body, 0)

    t_in_seq = pl.program_id(0) % tiles_per_seq
    hs = h_scr[...]
    u = (jnp.dot(hs, wc_ref[...], preferred_element_type=F32)
         * jnp.dot(hs, wh_ref[...], preferred_element_type=F32))
    u_scr[0:hl, :] = jnp.where(t_in_seq > 0, u[0:hl], 0.0)
    u_scr[hl:hl + tm, :] = u[hl:hl + tm]
    u_scr[hl + tm:hl + tm + hl, :] = jnp.where(t_in_seq < tiles_per_seq - 1, u[hl + tm:], 0.0)
    k = cw_ref.shape[0]
    acc = u_scr[hl - k // 2:hl - k // 2 + tm, :] * cw_ref[0:1, :]
    for j in range(1, k):
        off = hl - k // 2 + j
        acc += u_scr[off:off + tm, :] * cw_ref[j:j + 1, :]
    hm = h_scr[hl:hl + tm, :]
    bg = jnp.dot(hm, wb_ref[...], preferred_element_type=F32)
    gt = jnp.dot(hm, wg_ref[...], preferred_element_type=F32)
    z_ref[...] = (_silu(gt) * bg * acc).astype(z_ref.dtype)


def short_conv_mixer(x2, mod, layer, rows_per_seq, mod_row, g, w_in, conv_w, *, tm, tc):
    m, d = x2.shape
    w = w_in.shape[1] // 4
    tm = min(tm, rows_per_seq)
    tiles_per_seq = rows_per_seq // tm
    row = (lambda i: i // tiles_per_seq) if mod_row is None else (lambda i: mod_row)
    hb = tm // _SC_HALO
    last = m // _SC_HALO - 1
    nblk = w // tc
    w_spec = lambda part: pl.BlockSpec((d, tc), lambda i, j: (0, part * nblk + j))
    return pl.pallas_call(
        functools.partial(_short_conv_kernel, tiles_per_seq=tiles_per_seq),
        grid=(m // tm, nblk),
        in_specs=[
            pl.BlockSpec((tm, d), lambda i, j: (i, 0)),
            pl.BlockSpec((_SC_HALO, d), lambda i, j: (jnp.maximum(i * hb - 1, 0), 0)),
            pl.BlockSpec((_SC_HALO, d), lambda i, j: (jnp.minimum((i + 1) * hb, last), 0)),
            pl.BlockSpec((None, None, 1, d), lambda i, j: (layer, row(i), 0, 0)),
            pl.BlockSpec((None, None, 1, d), lambda i, j: (layer, row(i), 0, 1)),
            pl.BlockSpec((1, d), lambda i, j: (0, 0)),
            w_spec(0), w_spec(1), w_spec(2), w_spec(3),
            pl.BlockSpec((conv_w.shape[0], tc), lambda i, j: (0, j)),
        ],
        out_specs=pl.BlockSpec((tm, tc), lambda i, j: (i, j)),
        out_shape=jax.ShapeDtypeStruct((m, w), BF16),
        scratch_shapes=[pltpu.VMEM((tm + 2 * _SC_HALO, d), BF16),
                        pltpu.VMEM((tm + 2 * _SC_HALO, tc), F32)],
        compiler_params=_params(("parallel", "arbitrary")),
        name="short_conv",
    )(x2, x2, x2, mod, mod, g, w_in, w_in, w_in, w_in, conv_w)


def _even_columns(d_model):
    na_width = NA_HEADS * NA_HEAD_DIM
    ssd_width = SSD_GROUPS * SSD_HPG * SSD_HEAD_DIM
    cols = {"q": 0}
    cols["gate"] = cols["q"] + na_width
    cols["z"] = cols["gate"] + na_width
    cols["k"] = cols["z"] + ssd_width
    cols["v"] = cols["k"] + na_width
    cols["xbc"] = cols["v"] + na_width
    cols["dt"] = cols["xbc"] + ssd_width + 2 * SSD_GROUPS * SSD_STATE
    return cols


def _dt_weight_rows(w_in, cols):
    wdt = w_in[:, cols["dt"]:cols["dt"] + 2 * SSD_HEADS]
    wdt = wdt.reshape(-1, 2, SSD_GROUPS, SSD_HPG).transpose(2, 1, 3, 0).reshape(2 * SSD_HEADS, -1)
    pad = jnp.zeros((V7X_LANES - 2 * SSD_HEADS, wdt.shape[1]), wdt.dtype)
    return jnp.concatenate([wdt, pad], axis=0).astype(BF16)


def _per_group_rows(v):
    v = v.astype(F32).reshape(2, SSD_GROUPS, SSD_HPG).transpose(1, 0, 2).reshape(SSD_GROUPS, 2 * SSD_HPG, 1)
    return jnp.broadcast_to(v, (SSD_GROUPS, 2 * SSD_HPG, V7X_LANES))


def _forward(x, c, ctx, c_ctx, ada_w, ada_b, norm_g, na_ssd_w_in, ssd_conv_w, ssd_conv_b, ssd_a_log,
             ssd_dt_bias, ssd_d, ssd_norm_g, q_norm_g, k_norm_g, na_rpb, na_ssd_w_out, sc_w_in,
             sc_conv_w, sc_w_out, *, tm, tn, tm_sc, tc, tn_out):
    bsz, seq, d = x.shape
    nctx = ctx.shape[1]
    depth = ada_w.shape[0]
    cols = _even_columns(d)
    hp = SSD_HPG * SSD_HEAD_DIM

    cond = jnp.concatenate([c, c_ctx[None, :], jnp.zeros((8 - bsz - 1, d), F32)], axis=0)
    mod = adaln_all(cond, ada_w, ada_b).reshape(depth, 8, 1, 3 * d)
    ctx_row = bsz

    x2 = x.reshape(bsz * seq, d)
    ctx2 = ctx.reshape(bsz * nctx, d)
    for i in range(depth):
        update_ctx = any(j % 2 == 0 for j in range(i + 1, depth))
        needs_ctx = (i % 2 == 0) or update_ctx
        g = norm_g[i].reshape(1, d)
        if i % 2 == 0:
            e = i // 2
            w_in = na_ssd_w_in[e]
            w_main = w_in[:, :cols["dt"]].astype(BF16)
            wdt_t = _dt_weight_rows(w_in, cols)
            w_out = na_ssd_w_out[e].astype(BF16)
            p, dt_t = project(x2, mod, i, seq, None, g, w_main, wdt_t, tm=tm, tn=tn)
            pc, dtc_t = project(ctx2, mod, i, nctx, ctx_row, g, w_main, wdt_t, tm=tm, tn=tn)
            p3 = p.reshape(bsz, seq, -1)
            pc3 = pc.reshape(bsz, nctx, -1)
            qg = q_norm_g[e].reshape(1, -1)
            kg = k_norm_g[e].reshape(1, -1)
            cmat = expand_bias(na_rpb[e])
            ya = neighbourhood_attention(p3, pc3, cmat, qg, kg, cols)
            dtb8 = _per_group_rows(ssd_dt_bias[e])
            a8 = _per_group_rows(-jnp.exp(ssd_a_log[e].astype(F32)))
            dskip = jnp.repeat(ssd_d[e].astype(F32), SSD_HEAD_DIM).reshape(1, -1)
            sng = ssd_norm_g[e].reshape(1, -1)
            cw = ssd_conv_w[e]
            cb = ssd_conv_b[e].reshape(1, -1)
            h_zero = jnp.zeros((bsz, SSD_GROUPS, 2, SSD_STATE, hp), F32)
            ybc, h_ctx = ssd_mixer(pc3, dtc_t, cw, cb, dtb8, a8, dskip, sng, h_zero, cols)
            yb, _ = ssd_mixer(p3, dt_t, cw, cb, dtb8, a8, dskip, sng, h_ctx, cols)
            x2_new = out_project([ya.reshape(bsz * seq, -1), yb.reshape(bsz * seq, -1)], w_out, x2,
                                 mod, i, seq, None, tm=tm, tn=tn_out)
            if update_ctx:
                yac = context_attention(pc3, qg, kg, cols)
                ctx2 = out_project([yac.reshape(bsz * nctx, -1), ybc.reshape(bsz * nctx, -1)], w_out,
                                   ctx2, mod, i, nctx, ctx_row, tm=tm, tn=tn_out)
            x2 = x2_new
        else:
            o = i // 2
            w_in = sc_w_in[o].astype(BF16)
            w_out = sc_w_out[o].astype(BF16)
            z = short_conv_mixer(x2, mod, i, seq, None, g, w_in, sc_conv_w[o], tm=tm_sc, tc=tc)
            x2_new = out_project([z], w_out, x2, mod, i, seq, None, tm=tm, tn=tn_out)
            if update_ctx:
                zc = short_conv_mixer(ctx2, mod, i, nctx, ctx_row, g, w_in, sc_conv_w[o], tm=tm_sc, tc=tc)
                ctx2 = out_project([zc], w_out, ctx2, mod, i, nctx, ctx_row, tm=tm, tn=tn_out)
            x2 = x2_new
    return x2.reshape(bsz, seq, d)


def kernel(x, c, ctx, c_ctx, ada_w, ada_b, norm_g, na_ssd_w_in, ssd_conv_w, ssd_conv_b, ssd_a_log,
           ssd_dt_bias, ssd_d, ssd_norm_g, q_norm_g, k_norm_g, na_rpb, na_ssd_w_out, sc_w_in,
           sc_conv_w, sc_w_out):
    return _forward(x, c, ctx, c_ctx, ada_w, ada_b, norm_g, na_ssd_w_in, ssd_conv_w, ssd_conv_b,
                    ssd_a_log, ssd_dt_bias, ssd_d, ssd_norm_g, q_norm_g, k_norm_g, na_rpb,
                    na_ssd_w_out, sc_w_in, sc_conv_w, sc_w_out, tm=2048, tn=512, tm_sc=1024, tc=256, tn_out=256)
```

```python
import functools

import jax
import jax.numpy as jnp
import numpy as np
from jax import lax
from jax.experimental import pallas as pl
from jax.experimental.pallas import tpu as pltpu

F32 = jnp.float32
BF16 = jnp.bfloat16

EPS = 1e-6
GRID_W = 64
NA_HEADS = 16
NA_HEAD_DIM = 128
NA_KH = 8
NA_KW = 16
SSD_HEAD_DIM = 64
SSD_GROUPS = 8
SSD_HPG = 4
SSD_HEADS = SSD_GROUPS * SSD_HPG
SSD_STATE = 128
SSD_CONV = 5
SSD_CHUNK = 128
SC_CONV = 3

V7X_LANES = 128
V7X_VMEM_LIMIT = 56 * 1024 * 1024

MASKED = -1e30

NA_QROWS = 8
NA_KROWS = 16
_NA_SOFTMAX_ROWS = 32


def _silu(x):
    return x * (1.0 / (1.0 + jnp.exp(-x)))


def _softplus(x):
    return jnp.maximum(x, 0.0) + jnp.log(1.0 + jnp.exp(-jnp.abs(x)))


def _split3(a):
    hi = a.astype(BF16)
    r1 = a - hi.astype(F32)
    mid = r1.astype(BF16)
    lo = (r1 - mid.astype(F32)).astype(BF16)
    return hi, mid, lo


def _params(sem, vmem=V7X_VMEM_LIMIT):
    return pltpu.CompilerParams(dimension_semantics=sem, vmem_limit_bytes=vmem)


def _adaln_kernel(cond_ref, w_ref, b_ref, o_ref):
    s = _silu(cond_ref[...])
    w = w_ref[...]
    s_hi, s_mid, _ = _split3(s)
    w_hi, w_mid, _ = _split3(w)
    acc = jnp.dot(s_hi, w_hi, preferred_element_type=F32)
    acc += jnp.dot(s_hi, w_mid, preferred_element_type=F32)
    acc += jnp.dot(s_mid, w_hi, preferred_element_type=F32)
    o_ref[...] = acc + b_ref[...]


def adaln_all(cond, ada_w, ada_b, tn=768):
    depth, d, n = ada_w.shape
    rows = cond.shape[0]
    return pl.pallas_call(
        _adaln_kernel,
        grid=(depth, n // tn),
        in_specs=[
            pl.BlockSpec((rows, d), lambda l, j: (0, 0)),
            pl.BlockSpec((None, d, tn), lambda l, j: (l, 0, j)),
            pl.BlockSpec((None, 1, tn), lambda l, j: (l, 0, j)),
        ],
        out_specs=pl.BlockSpec((None, rows, tn), lambda l, j: (l, 0, j)),
        out_shape=jax.ShapeDtypeStruct((depth, rows, n), F32),
        compiler_params=_params(("parallel", "parallel")),
        name="adaln",
    )(cond, ada_w, ada_b.reshape(depth, 1, n))


_PROJ_ROWS = 128


_HEAD_NORM_ROWS = 256


def _proj_kernel(x_ref, shift_ref, scale_ref, g_ref, w_ref, wdt_ref, qg_ref, kg_ref, o_ref, odt_ref, h_scr,
                 *, q_tiles, k_tiles):
    tm = x_ref.shape[0]
    tn = o_ref.shape[1]
    j = pl.program_id(1)

    @pl.when(j == 0)
    def _():
        gain = g_ref[...] * (1.0 + scale_ref[...])
        shift = shift_ref[...]

        def body(r, carry):
            rows = pl.ds(pl.multiple_of(r * _PROJ_ROWS, _PROJ_ROWS), _PROJ_ROWS)
            x = x_ref[rows, :]
            ms = jnp.mean(x * x, axis=-1, keepdims=True)
            h_scr[rows, :] = (x * lax.rsqrt(ms + EPS) * gain + shift).astype(BF16)
            return carry

        lax.fori_loop(0, tm // _PROJ_ROWS, body, 0)
        odt_ref[...] = lax.dot_general(wdt_ref[...], h_scr[...], (((1,), (1,)), ((), ())),
                                       preferred_element_type=F32)

    is_q = (j >= q_tiles[0]) & (j < q_tiles[1])
    is_k = (j >= k_tiles[0]) & (j < k_tiles[1])

    @pl.when(is_q | is_k)
    def _():
        acc = jnp.dot(h_scr[...], w_ref[...], preferred_element_type=F32)
        gain = jnp.where(is_q, qg_ref[...], kg_ref[...])
        step = min(_HEAD_NORM_ROWS, tm)
        for r in range(tm // step):
            for hb in range(tn // NA_HEAD_DIM):
                lanes = slice(hb * NA_HEAD_DIM, (hb + 1) * NA_HEAD_DIM)
                a = acc[r * step:(r + 1) * step, lanes]
                ms = jnp.mean(a * a, axis=-1, keepdims=True)
                o_ref[r * step:(r + 1) * step, lanes] = (a * lax.rsqrt(ms + EPS) * gain[:, lanes]).astype(o_ref.dtype)

    @pl.when(jnp.logical_not(is_q | is_k))
    def _():
        o_ref[...] = jnp.dot(h_scr[...], w_ref[...], preferred_element_type=F32).astype(o_ref.dtype)


def project_even(x2, mod, layer, rows_per_seq, mod_row, g, w, wdt_t, qg_t, kg_t, cols, *, tm, tn):
    m, d = x2.shape
    n = w.shape[1]
    tm = min(tm, rows_per_seq)
    tiles_per_seq = rows_per_seq // tm
    row = (lambda i: i // tiles_per_seq) if mod_row is None else (lambda i: mod_row)
    na_width = NA_HEADS * NA_HEAD_DIM
    q_tiles = (cols["q"] // tn, (cols["q"] + na_width) // tn)
    k_tiles = (cols["k"] // tn, (cols["k"] + na_width) // tn)
    return pl.pallas_call(
        functools.partial(_proj_kernel, q_tiles=q_tiles, k_tiles=k_tiles),
        grid=(m // tm, n // tn),
        in_specs=[
            pl.BlockSpec((tm, d), lambda i, j: (i, 0)),
            pl.BlockSpec((None, None, 1, d), lambda i, j: (layer, row(i), 0, 0)),
            pl.BlockSpec((None, None, 1, d), lambda i, j: (layer, row(i), 0, 1)),
            pl.BlockSpec((1, d), lambda i, j: (0, 0)),
            pl.BlockSpec((d, tn), lambda i, j: (0, j)),
            pl.BlockSpec((wdt_t.shape[0], d), lambda i, j: (0, 0)),
            pl.BlockSpec((1, tn), lambda i, j: (0, 0)),
            pl.BlockSpec((1, tn), lambda i, j: (0, 0)),
        ],
        out_specs=[pl.BlockSpec((tm, tn), lambda i, j: (i, j)),
                   pl.BlockSpec((wdt_t.shape[0], tm), lambda i, j: (0, i))],
        out_shape=[jax.ShapeDtypeStruct((m, n), BF16),
                   jax.ShapeDtypeStruct((wdt_t.shape[0], m), F32)],
        scratch_shapes=[pltpu.VMEM((tm, d), BF16)],
        compiler_params=_params(("parallel", "arbitrary")),
        name="proj_dt",
    )(x2, mod, mod, g, w, wdt_t, qg_t, kg_t)


def _bias_kernel(rpb_ref, o_ref):
    h = pl.program_id(0)
    qc = lax.broadcasted_iota(jnp.int32, (GRID_W, GRID_W), 0)
    kc = lax.broadcasted_iota(jnp.int32, (GRID_W, GRID_W), 1)
    start = jnp.clip(qc - NA_KW // 2, 0, GRID_W - NA_KW)
    in_win = (kc >= start) & (kc < start + NA_KW)
    dc = jnp.clip(kc - qc, -(NA_KW - 1), NA_KW - 1) + NA_KW - 1
    for dr in range(2 * NA_KH - 1):
        acc = jnp.zeros((GRID_W, GRID_W), F32)
        for j in range(2 * NA_KW - 1):
            acc = jnp.where(dc == j, rpb_ref[h, dr * (2 * NA_KW - 1) + j], acc)
        o_ref[dr] = jnp.where(in_win, acc, MASKED)


def expand_bias(rpb):
    nh = rpb.shape[0]
    ndr, ndc = 2 * NA_KH - 1, 2 * NA_KW - 1
    return pl.pallas_call(
        _bias_kernel,
        grid=(nh,),
        in_specs=[pl.BlockSpec(memory_space=pltpu.SMEM)],
        out_specs=pl.BlockSpec((None, ndr, GRID_W, GRID_W), lambda h: (h, 0, 0, 0)),
        out_shape=jax.ShapeDtypeStruct((nh, ndr, GRID_W, GRID_W), F32),
        compiler_params=_params(("arbitrary",)),
        name="rpb_expand",
    )(rpb.reshape(nh, ndr * ndc))


def _window_rows(case, qr, n_rows):
    if case == 1:
        lo, dr0 = qr, NA_KH - 1 - NA_KH // 2
    elif case == 0:
        lo = max(qr - NA_KH // 2, 0)
        dr0 = lo - qr + NA_KH - 1
    else:
        r = n_rows - NA_QROWS + qr
        rs = min(r - NA_KH // 2, n_rows - NA_KH)
        lo = rs - (n_rows - NA_KROWS)
        dr0 = rs - r + NA_KH - 1
    return lo, dr0


def _na_kernel(q_ref, gate_ref, k_ref, v_ref, kc_ref, vc_ref, cmat_ref, o_ref,
               v1_scr, vc1_scr, bias_scr, s0_scr, s1_scr, p0_scr, p1_scr, *, n_rows):
    seq = k_ref.shape[0]
    nctx = kc_ref.shape[0]
    dh = NA_HEAD_DIM
    nq = NA_QROWS * GRID_W
    nk = NA_KROWS * GRID_W
    nb = n_rows // NA_QROWS

    @pl.when(pl.program_id(1) == 0)
    def _():
        masked = jnp.full((GRID_W, GRID_W), MASKED, F32)
        for case in range(3):
            for qr in range(NA_QROWS):
                lo, dr0 = _window_rows(case, qr, n_rows)
                strip = [cmat_ref[dr0 + kr - lo] if lo <= kr < lo + NA_KH else masked
                         for kr in range(NA_KROWS)]
                bias_scr[case, qr * GRID_W:(qr + 1) * GRID_W, :] = jnp.concatenate(strip, axis=1)

    @pl.when((pl.program_id(0) == 0) & (pl.program_id(1) == 0))
    def _():
        v1_scr[:, dh:2 * dh] = jnp.ones((seq, dh), BF16)
        vc1_scr[:, dh:2 * dh] = jnp.ones((nctx, dh), BF16)

    v1_scr[:, 0:dh] = v_ref[...]
    vc1_scr[:, 0:dh] = vc_ref[...]

    nt = (((1,), (1,)), ((), ()))

    def block_rows(i):
        qrows = pl.ds(pl.multiple_of(i * nq, nq), nq)
        kstart = jnp.clip(i * NA_QROWS - NA_KH // 2, 0, n_rows - NA_KROWS)
        krows = pl.ds(pl.multiple_of(kstart * GRID_W, GRID_W), nk)
        return qrows, krows

    def scores(i, s_scr):
        i = jnp.minimum(i, nb - 1)
        qrows, krows = block_rows(i)
        qn = q_ref[qrows, :]
        case = jnp.where(i == 0, 0, jnp.where(i == nb - 1, 2, 1))
        s_scr[:, 0:nctx] = lax.dot_general(qn, kc_ref[...], nt, preferred_element_type=F32)
        s_scr[:, nctx:nctx + nk] = (lax.dot_general(qn, k_ref[krows, :], nt, preferred_element_type=F32)
                                    + bias_scr[case])

    def softmax(s_scr, p_scr):
        for g in range(nq // _NA_SOFTMAX_ROWS):
            rows = slice(g * _NA_SOFTMAX_ROWS, (g + 1) * _NA_SOFTMAX_ROWS)
            s = s_scr[rows, :]
            p_scr[rows, :] = jnp.exp(s - jnp.max(s, axis=-1, keepdims=True)).astype(BF16)

    def values(i, p_scr):
        qrows, krows = block_rows(i)
        o = jnp.dot(p_scr[:, 0:nctx], vc1_scr[...], preferred_element_type=F32)
        o += jnp.dot(p_scr[:, nctx:nctx + nk], v1_scr[krows, :], preferred_element_type=F32)
        gated = o[:, 0:dh] * (1.0 / o[:, dh:2 * dh]) * _silu(gate_ref[qrows, :].astype(F32))
        o_ref[qrows, :] = gated.astype(o_ref.dtype)

    scores(0, s0_scr)
    softmax(s0_scr, p0_scr)
    scores(1, s1_scr)

    def pair(jj, carry):
        j = 2 * jj
        values(j, p0_scr)
        softmax(s1_scr, p1_scr)
        scores(j + 2, s0_scr)
        values(j + 1, p1_scr)
        softmax(s0_scr, p0_scr)
        scores(j + 3, s1_scr)
        return carry

    lax.fori_loop(0, nb // 2, pair, 0)


def neighbourhood_attention(p, pc, cmat, cols):
    bsz, seq, _ = p.shape
    nctx = pc.shape[1]
    n_rows = seq // GRID_W
    nb = n_rows // NA_QROWS
    assert n_rows >= NA_KROWS and n_rows % NA_QROWS == 0 and nb % 2 == 0 and nctx % V7X_LANES == 0
    nq = NA_QROWS * GRID_W
    nk = NA_KROWS * GRID_W
    dh = NA_HEAD_DIM
    cq, cg, ck, cv = (cols[k] // dh for k in ("q", "gate", "k", "v"))
    slab = lambda c: pl.BlockSpec((None, seq, dh), lambda h, b: (b, 0, c + h))
    cslab = lambda c: pl.BlockSpec((None, nctx, dh), lambda h, b: (b, 0, c + h))
    return pl.pallas_call(
        functools.partial(_na_kernel, n_rows=n_rows),
        grid=(NA_HEADS, bsz),
        in_specs=[
            slab(cq), slab(cg), slab(ck), slab(cv), cslab(ck), cslab(cv),
            pl.BlockSpec((None, 2 * NA_KH - 1, GRID_W, GRID_W), lambda h, b: (h, 0, 0, 0)),
        ],
        out_specs=pl.BlockSpec((None, seq, dh), lambda h, b: (b, 0, h)),
        out_shape=jax.ShapeDtypeStruct((bsz, seq, NA_HEADS * dh), BF16),
        scratch_shapes=[
            pltpu.VMEM((seq, 2 * dh), BF16),
            pltpu.VMEM((nctx, 2 * dh), BF16),
            pltpu.VMEM((3, nq, nk), F32),
            pltpu.VMEM((nq, nctx + nk), F32),
            pltpu.VMEM((nq, nctx + nk), F32),
            pltpu.VMEM((nq, nctx + nk), BF16),
            pltpu.VMEM((nq, nctx + nk), BF16),
        ],
        compiler_params=_params(("arbitrary", "arbitrary")),
        name="nbr_attention",
    )(p, p, p, p, pc, pc, cmat)


def _ctx_attn_kernel(q_ref, gate_ref, k_ref, v_ref, o_ref):
    s = lax.dot_general(q_ref[...], k_ref[...], (((1,), (1,)), ((), ())), preferred_element_type=F32)
    m = jnp.max(s, axis=-1, keepdims=True)
    pr = jnp.exp(s - m)
    l = jnp.sum(pr, axis=-1, keepdims=True)
    o = jnp.dot(pr.astype(BF16), v_ref[...], preferred_element_type=F32)
    o_ref[...] = (o * (1.0 / l) * _silu(gate_ref[...].astype(F32))).astype(o_ref.dtype)


def context_attention(pc, cols):
    bsz, nctx, _ = pc.shape
    dh = NA_HEAD_DIM
    cq, cg, ck, cv = (cols[k] // dh for k in ("q", "gate", "k", "v"))
    blk = lambda c: pl.BlockSpec((None, nctx, dh), lambda b, h: (b, 0, c + h))
    return pl.pallas_call(
        _ctx_attn_kernel,
        grid=(bsz, NA_HEADS),
        in_specs=[blk(cq), blk(cg), blk(ck), blk(cv)],
        out_specs=pl.BlockSpec((None, nctx, dh), lambda b, h: (b, 0, h)),
        out_shape=jax.ShapeDtypeStruct((bsz, nctx, NA_HEADS * dh), BF16),
        compiler_params=_params(("parallel", "parallel")),
        name="ctx_attention",
    )(pc, pc, pc, pc)


_CONV_HALO = 16


def _conv_silu_chunk(src_ref, c, nc, w_ref, b_ref, stage_ref):
    t = SSD_CHUNK
    hl = _CONV_HALO
    length = src_ref.shape[0]
    start = pl.multiple_of(c * t, t)
    prev = src_ref[pl.ds(pl.multiple_of(jnp.maximum(start - hl, 0), hl), hl), :].astype(F32)
    nxt = src_ref[pl.ds(pl.multiple_of(jnp.minimum(start + t, length - hl), hl), hl), :].astype(F32)
    stage_ref[0:hl, :] = jnp.where(c > 0, prev, 0.0)
    stage_ref[hl:hl + t, :] = src_ref[pl.ds(start, t), :].astype(F32)
    stage_ref[hl + t:hl + t + hl, :] = jnp.where(c < nc - 1, nxt, 0.0)
    k = w_ref.shape[0]
    acc = jnp.zeros((t, src_ref.shape[1]), F32) + b_ref[...]
    for j in range(k):
        off = hl - k // 2 + j
        acc += stage_ref[off:off + t, :] * w_ref[j:j + 1, :]
    return _silu(acc)


def _ssd_kernel(x_ref, bm_ref, cm_ref, z_ref, dtt_ref, wx_ref, bx_ref, wb_ref, bb_ref, wc_ref, bc_ref,
                dtb_ref, a_ref, dskip_ref, ng_ref, h0_ref, y_ref, hfin_ref,
                xs_scr, bmc_scr, cmc_scr, yacc_scr, stx_scr, stb_scr, rows_scr, col_scr, fac_scr):
    t = SSD_CHUNK
    length = x_ref.shape[0]
    nc = length // t
    hp = SSD_HPG * SSD_HEAD_DIM

    def conv_chunk(c, side):
        rows = pl.ds(pl.multiple_of(c * t, t), t)
        xs_scr[rows, :] = _conv_silu_chunk(x_ref, c, nc, wx_ref, bx_ref, stx_scr.at[side]).astype(BF16)
        bmc_scr[rows, :] = _conv_silu_chunk(bm_ref, c, nc, wb_ref, bb_ref, stb_scr.at[2 * side]).astype(BF16)
        cmc_scr[rows, :] = _conv_silu_chunk(cm_ref, c, nc, wc_ref, bc_ref,
                                            stb_scr.at[2 * side + 1]).astype(BF16)

    conv_chunk(0, 0)
    conv_chunk(nc - 1, 1)
    hfin_ref[...] = h0_ref[...]

    ii = lax.broadcasted_iota(jnp.int32, (t, t), 0)
    jj = lax.broadcasted_iota(jnp.int32, (t, t), 1)
    lower = ii >= jj
    diag = ii == jj
    indicator = lambda cond: jnp.where(cond, 1.0, 0.0).astype(BF16)
    tri2_b = jnp.concatenate([indicator(ii <= jj), indicator(lower)], axis=1)
    fwd_rows = lax.broadcasted_iota(jnp.int32, (2 * SSD_HPG, t), 0) < SSD_HPG
    er = lax.broadcasted_iota(jnp.int32, (2 * SSD_HPG, hp), 0)
    el = lax.broadcasted_iota(jnp.int32, (2 * SSD_HPG, hp), 1) // SSD_HEAD_DIM
    expand = [indicator(er == el + d * SSD_HPG) for d in range(2)]
    lane_head = lax.broadcasted_iota(jnp.int32, (t, hp), 1) // SSD_HEAD_DIM
    head_lanes = [indicator(lane_head == r) for r in range(SSD_HPG)]
    nt = (((1,), (1,)), ((), ()))
    tn = (((0,), (0,)), ((), ()))

    def chunk_rows(c):
        return pl.ds(pl.multiple_of(c * t, t), t)

    def decay_terms(c):
        dt8 = _softplus(dtt_ref[:, chunk_rows(c)] + dtb_ref[...])
        dta8 = dt8 * a_ref[...]
        hi = dta8.astype(BF16).astype(F32)
        rest = dta8 - hi
        mid = rest.astype(BF16).astype(F32)
        terms = jnp.concatenate([hi, mid, rest - mid], axis=0).astype(BF16)
        cs3 = jnp.dot(terms, tri2_b, preferred_element_type=F32)
        cs2 = cs3[0:8] + cs3[8:16] + cs3[16:24]
        cs_row = jnp.where(fwd_rows, cs2[:, 0:t], cs2[:, t:2 * t])
        return dt8, cs_row, dt8.T, cs_row.T

    def factors(d, dt_col, cs_col):
        edge = t - 1 if d == 0 else 0
        to_end_col = jnp.exp(jnp.minimum(cs_col[edge:edge + 1, :] - cs_col, 0.0)) * dt_col
        ecs = jnp.dot(jnp.exp(cs_col).astype(BF16), expand[d], preferred_element_type=F32)
        to_end = jnp.dot(to_end_col.astype(BF16), expand[d], preferred_element_type=F32)
        return ecs, to_end

    def prepare(k, buf):
        dt8, cs_row, dt_col, cs_col = decay_terms(jnp.minimum(k, nc - 1))
        rows_scr[buf, 0:8, :] = dt8
        rows_scr[buf, 8:16, :] = cs_row
        col_scr[buf] = cs_col
        ecs, to_end = factors(0, dt_col, cs_col)
        fac_scr[buf, 0] = ecs
        fac_scr[buf, 1] = to_end
        _, _, dt_col_b, cs_col_b = decay_terms(jnp.maximum(nc - 1 - k, 0))
        ecs_b, to_end_b = factors(1, dt_col_b, cs_col_b)
        fac_scr[buf, 2] = ecs_b
        fac_scr[buf, 3] = to_end_b

    def intra(c, buf):
        dt8 = rows_scr[buf, 0:8, :]
        cs_row = rows_scr[buf, 8:16, :]
        cs_col = col_scr[buf]
        rows = chunk_rows(c)
        xs = xs_scr[rows, :]
        cb = lax.dot_general(cmc_scr[rows, :], bmc_scr[rows, :], nt, preferred_element_type=F32)
        cb_diag = jnp.where(diag, cb, 0.0)
        ws, xbd = [], []
        for r in range(SSD_HPG):
            f, b = r, SSD_HPG + r
            seg = jnp.where(lower, cs_col[:, f:f + 1] - cs_row[f:f + 1, :],
                            cs_col[:, b:b + 1] - cs_row[b:b + 1, :])
            dt_sel = jnp.where(lower, dt8[f:f + 1, :], dt8[b:b + 1, :])
            w = cb * (jnp.exp(seg) * dt_sel) + cb_diag * dt8[b:b + 1, :]
            ws.append(w.astype(BF16))
            xbd.append(xs * head_lanes[r])
        return jnp.dot(jnp.concatenate(ws, axis=1), jnp.concatenate(xbd, axis=0), preferred_element_type=F32)

    def state(c, d, buf):
        rows = chunk_rows(c)
        edge = t - 1 if d == 0 else 0
        ecs = fac_scr[buf, 2 * d]
        to_end = fac_scr[buf, 2 * d + 1]
        xs = xs_scr[rows, :]
        h = hfin_ref[d]
        y = jnp.dot(cmc_scr[rows, :], h.astype(BF16), preferred_element_type=F32) * ecs
        xw = (xs.astype(F32) * to_end).astype(BF16)
        hfin_ref[d] = h * ecs[edge:edge + 1, :] + lax.dot_general(bmc_scr[rows, :], xw, tn,
                                                                  preferred_element_type=F32)
        return y

    def finish(c, y):
        rows = chunk_rows(c)
        y = y + yacc_scr[rows, :] + dskip_ref[...] * xs_scr[rows, :].astype(F32)
        yz = y * _silu(z_ref[rows, :].astype(F32))
        yz = yz * lax.rsqrt(jnp.mean(yz * yz, axis=-1, keepdims=True) + EPS)
        y_ref[rows, :] = (yz * ng_ref[...]).astype(y_ref.dtype)

    def first_half(k, buf):
        yacc_scr[chunk_rows(k), :] = intra(k, buf) + state(k, 0, buf)
        yacc_scr[chunk_rows(nc - 1 - k), :] = state(nc - 1 - k, 1, buf)
        prepare(k + 1, 1 - buf)
        conv_chunk(k + 1, 0)
        conv_chunk(nc - 2 - k, 1)

    def second_half(k, buf):
        finish(k, intra(k, buf) + state(k, 0, buf))
        finish(nc - 1 - k, state(nc - 1 - k, 1, buf))
        prepare(k + 1, 1 - buf)

    def walk(body, start, stop):
        if start % 2 == 0 and (stop - start) % 2 == 0 and stop - start > 2:
            def pair(kk, carry):
                body(2 * kk, 0)
                body(2 * kk + 1, 1)
                return carry

            lax.fori_loop(start // 2, stop // 2, pair, 0)
        else:
            for k in range(start, stop):
                body(jnp.int32(k), k % 2)

    prepare(0, 0)
    walk(first_half, 0, nc // 2)
    walk(second_half, nc // 2, nc)


def ssd_mixer(p, dt_t, conv_w, conv_b, dtb8, a8, dskip, norm_g, h0, cols):
    bsz, length, _ = p.shape
    hp = SSD_HPG * SSD_HEAD_DIM
    ns = SSD_STATE
    width = SSD_GROUPS * hp
    k = conv_w.shape[0]
    cx = cols["xbc"] // hp
    cb_ = (cols["xbc"] + width) // ns
    cc_ = (cols["xbc"] + width + SSD_GROUPS * ns) // ns
    cz = cols["z"] // hp
    st_rows = SSD_CHUNK + 2 * _CONV_HALO
    return pl.pallas_call(
        _ssd_kernel,
        grid=(bsz, SSD_GROUPS),
        in_specs=[
            pl.BlockSpec((None, length, hp), lambda b, g: (b, 0, cx + g)),
            pl.BlockSpec((None, length, ns), lambda b, g: (b, 0, cb_ + g)),
            pl.BlockSpec((None, length, ns), lambda b, g: (b, 0, cc_ + g)),
            pl.BlockSpec((None, length, hp), lambda b, g: (b, 0, cz + g)),
            pl.BlockSpec((2 * SSD_HPG, length), lambda b, g: (g, b)),
            pl.BlockSpec((k, hp), lambda b, g: (0, g)),
            pl.BlockSpec((1, hp), lambda b, g: (0, g)),
            pl.BlockSpec((k, ns), lambda b, g: (0, width // ns + g)),
            pl.BlockSpec((1, ns), lambda b, g: (0, width // ns + g)),
            pl.BlockSpec((k, ns), lambda b, g: (0, width // ns + SSD_GROUPS + g)),
            pl.BlockSpec((1, ns), lambda b, g: (0, width // ns + SSD_GROUPS + g)),
            pl.BlockSpec((None, 2 * SSD_HPG, V7X_LANES), lambda b, g: (g, 0, 0)),
            pl.BlockSpec((None, 2 * SSD_HPG, V7X_LANES), lambda b, g: (g, 0, 0)),
            pl.BlockSpec((1, hp), lambda b, g: (0, g)),
            pl.BlockSpec((1, hp), lambda b, g: (0, g)),
            pl.BlockSpec((None, None, 2, ns, hp), lambda b, g: (b, g, 0, 0, 0)),
        ],
        out_specs=[
            pl.BlockSpec((None, length, hp), lambda b, g: (b, 0, g)),
            pl.BlockSpec((None, None, 2, ns, hp), lambda b, g: (b, g, 0, 0, 0)),
        ],
        out_shape=[
            jax.ShapeDtypeStruct((bsz, length, width), BF16),
            jax.ShapeDtypeStruct((bsz, SSD_GROUPS, 2, ns, hp), F32),
        ],
        scratch_shapes=[
            pltpu.VMEM((length, hp), BF16),
            pltpu.VMEM((length, ns), BF16),
            pltpu.VMEM((length, ns), BF16),
            pltpu.VMEM((length, hp), F32),
            pltpu.VMEM((2, st_rows, hp), F32),
            pltpu.VMEM((4, st_rows, ns), F32),
            pltpu.VMEM((2, 4 * SSD_HPG, SSD_CHUNK), F32),
            pltpu.VMEM((2, SSD_CHUNK, 2 * SSD_HPG), F32),
            pltpu.VMEM((2, 4, SSD_CHUNK, hp), F32),
        ],
        compiler_params=_params(("parallel", "parallel")),
        name="ssd_mixer",
    )(p, p, p, p, dt_t, conv_w, conv_b, conv_w, conv_b, conv_w, conv_b, dtb8, a8, dskip, norm_g, h0)


def _out_kernel(*refs, n_in):
    a_refs, w_refs = refs[:n_in], refs[n_in:2 * n_in]
    x_ref, gate_ref, o_ref = refs[2 * n_in:]
    acc = jnp.dot(a_refs[0][...], w_refs[0][...].astype(BF16), preferred_element_type=F32)
    for a_ref, w_ref in zip(a_refs[1:], w_refs[1:]):
        acc += jnp.dot(a_ref[...], w_ref[...].astype(BF16), preferred_element_type=F32)
    o_ref[...] = x_ref[...] + gate_ref[...] * acc


def out_project(acts, w_stack, w_idx, x2, mod, layer, rows_per_seq, mod_row, *, tm, tn):
    m, d = x2.shape
    kw = acts[0].shape[1]
    assert all(a.shape[1] == kw for a in acts) and w_stack.shape[1] == kw * len(acts)
    w_out = w_stack
    tm = min(tm, rows_per_seq)
    tiles_per_seq = rows_per_seq // tm
    row = (lambda i: i // tiles_per_seq) if mod_row is None else (lambda i: mod_row)
    w_spec = lambda n: pl.BlockSpec((None, kw, tn), lambda i, j: (w_idx, n, j))
    return pl.pallas_call(
        functools.partial(_out_kernel, n_in=len(acts)),
        grid=(m // tm, d // tn),
        in_specs=[pl.BlockSpec((tm, kw), lambda i, j: (i, 0)) for _ in acts]
        + [w_spec(n) for n in range(len(acts))]
        + [pl.BlockSpec((tm, tn), lambda i, j: (i, j)),
           pl.BlockSpec((None, None, 1, tn), lambda i, j: (layer, row(i), 0, 2 * (d // tn) + j))],
        out_specs=pl.BlockSpec((tm, tn), lambda i, j: (i, j)),
        out_shape=jax.ShapeDtypeStruct((m, d), F32),
        compiler_params=_params(("parallel", "arbitrary")),
        name="out_proj",
    )(*acts, *([w_out] * len(acts)), x2, mod)


_SC_HALO = 16


def _short_conv_kernel(x_ref, xp_ref, xn_ref, shift_ref, scale_ref, g_ref, wb_ref, wc_ref, wh_ref, wg_ref,
                       cw_ref, z_ref, h_scr, u_scr, *, tiles_per_seq):
    tm = x_ref.shape[0]
    hl = _SC_HALO

    @pl.when(pl.program_id(1) == 0)
    def _():
        gain = g_ref[...] * (1.0 + scale_ref[...])
        shift = shift_ref[...]

        def norm(x):
            ms = jnp.mean(x * x, axis=-1, keepdims=True)
            return (x * lax.rsqrt(ms + EPS) * gain + shift).astype(BF16)

        h_scr[0:hl, :] = norm(xp_ref[...])
        h_scr[hl + tm:hl + tm + hl, :] = norm(xn_ref[...])

        def body(r, carry):
            rows = pl.multiple_of(r * _PROJ_ROWS, _PROJ_ROWS)
            h_scr[pl.ds(hl + rows, _PROJ_ROWS), :] = norm(x_ref[pl.ds(rows, _PROJ_ROWS), :])
            return carry

        lax.fori_loop(0, tm // _PROJ_ROWS, body, 0)

    t_in_seq = pl.program_id(0) % tiles_per_seq
    hs = h_scr[...]
    u = (jnp.dot(hs, wc_ref[...].astype(BF16), preferred_element_type=F32)
         * jnp.dot(hs, wh_ref[...].astype(BF16), preferred_element_type=F32))
    u_scr[0:hl, :] = jnp.where(t_in_seq > 0, u[0:hl], 0.0)
    u_scr[hl:hl + tm, :] = u[hl:hl + tm]
    u_scr[hl + tm:hl + tm + hl, :] = jnp.where(t_in_seq < tiles_per_seq - 1, u[hl + tm:], 0.0)
    k = cw_ref.shape[0]
    acc = u_scr[hl - k // 2:hl - k // 2 + tm, :] * cw_ref[0:1, :]
    for j in range(1, k):
        off = hl - k // 2 + j
        acc += u_scr[off:off + tm, :] * cw_ref[j:j + 1, :]
    hm = h_scr[hl:hl + tm, :]
    bg = jnp.dot(hm, wb_ref[...].astype(BF16), preferred_element_type=F32)
    gt = jnp.dot(hm, wg_ref[...].astype(BF16), preferred_element_type=F32)
    z_ref[...] = (_silu(gt) * bg * acc).astype(z_ref.dtype)


def short_conv_mixer(x2, mod, layer, rows_per_seq, mod_row, g, w_stack, w_idx, conv_w, *, tm, tc):
    m, d = x2.shape
    w_in = w_stack
    w = w_in.shape[2] // 4
    tm = min(tm, rows_per_seq)
    tiles_per_seq = rows_per_seq // tm
    row = (lambda i: i // tiles_per_seq) if mod_row is None else (lambda i: mod_row)
    hb = tm // _SC_HALO
    last = m // _SC_HALO - 1
    nblk = w // tc
    w_spec = lambda part: pl.BlockSpec((None, d, tc), lambda i, j: (w_idx, 0, part * nblk + j))
    return pl.pallas_call(
        functools.partial(_short_conv_kernel, tiles_per_seq=tiles_per_seq),
        grid=(m // tm, nblk),
        in_specs=[
            pl.BlockSpec((tm, d), lambda i, j: (i, 0)),
            pl.BlockSpec((_SC_HALO, d), lambda i, j: (jnp.maximum(i * hb - 1, 0), 0)),
            pl.BlockSpec((_SC_HALO, d), lambda i, j: (jnp.minimum((i + 1) * hb, last), 0)),
            pl.BlockSpec((None, None, 1, d), lambda i, j: (layer, row(i), 0, 0)),
            pl.BlockSpec((None, None, 1, d), lambda i, j: (layer, row(i), 0, 1)),
            pl.BlockSpec((1, d), lambda i, j: (0, 0)),
            w_spec(0), w_spec(1), w_spec(2), w_spec(3),
            pl.BlockSpec((conv_w.shape[0], tc), lambda i, j: (0, j)),
        ],
        out_specs=pl.BlockSpec((tm, tc), lambda i, j: (i, j)),
        out_shape=jax.ShapeDtypeStruct((m, w), BF16),
        scratch_shapes=[pltpu.VMEM((tm + 2 * _SC_HALO, d), BF16),
                        pltpu.VMEM((tm + 2 * _SC_HALO, tc), F32)],
        compiler_params=_params(("parallel", "arbitrary")),
        name="short_conv",
    )(x2, x2, x2, mod, mod, g, w_in, w_in, w_in, w_in, conv_w)


def _even_columns(d_model):
    na_width = NA_HEADS * NA_HEAD_DIM
    ssd_width = SSD_GROUPS * SSD_HPG * SSD_HEAD_DIM
    cols = {"q": 0}
    cols["gate"] = cols["q"] + na_width
    cols["z"] = cols["gate"] + na_width
    cols["k"] = cols["z"] + ssd_width
    cols["v"] = cols["k"] + na_width
    cols["xbc"] = cols["v"] + na_width
    cols["dt"] = cols["xbc"] + ssd_width + 2 * SSD_GROUPS * SSD_STATE
    return cols


def _dt_weight_rows(w_in, cols):
    wdt = w_in[:, cols["dt"]:cols["dt"] + 2 * SSD_HEADS]
    wdt = wdt.reshape(-1, 2, SSD_GROUPS, SSD_HPG).transpose(2, 1, 3, 0).reshape(2 * SSD_HEADS, -1)
    pad = jnp.zeros((V7X_LANES - 2 * SSD_HEADS, wdt.shape[1]), wdt.dtype)
    return jnp.concatenate([wdt, pad], axis=0).astype(BF16)


def _per_group_rows(v):
    v = v.astype(F32).reshape(2, SSD_GROUPS, SSD_HPG).transpose(1, 0, 2).reshape(SSD_GROUPS, 2 * SSD_HPG, 1)
    return jnp.broadcast_to(v, (SSD_GROUPS, 2 * SSD_HPG, V7X_LANES))


def _forward(x, c, ctx, c_ctx, ada_w, ada_b, norm_g, na_ssd_w_in, ssd_conv_w, ssd_conv_b, ssd_a_log,
             ssd_dt_bias, ssd_d, ssd_norm_g, q_norm_g, k_norm_g, na_rpb, na_ssd_w_out, sc_w_in,
             sc_conv_w, sc_w_out, *, tm, tn, tm_sc, tc, tm_out_even):
    bsz, seq, d = x.shape
    nctx = ctx.shape[1]
    depth = ada_w.shape[0]
    cols = _even_columns(d)
    hp = SSD_HPG * SSD_HEAD_DIM

    cond = jnp.concatenate([c, c_ctx[None, :], jnp.zeros((8 - bsz - 1, d), F32)], axis=0)
    mod = adaln_all(cond, ada_w, ada_b).reshape(depth, 8, 1, 3 * d)
    ctx_row = bsz

    x2 = x.reshape(bsz * seq, d)
    ctx2 = ctx.reshape(bsz * nctx, d)
    for i in range(depth):
        update_ctx = any(j % 2 == 0 for j in range(i + 1, depth))
        g = norm_g[i].reshape(1, d)
        if i % 2 == 0:
            e = i // 2
            w_in = na_ssd_w_in[e]
            w_main = w_in[:, :cols["dt"]].astype(BF16)
            wdt_t = _dt_weight_rows(w_in, cols)
            heads_per_tile = tn // NA_HEAD_DIM
            qg_t = jnp.tile(q_norm_g[e].astype(F32) * NA_HEAD_DIM ** -0.5, heads_per_tile).reshape(1, tn)
            kg_t = jnp.tile(k_norm_g[e].astype(F32), heads_per_tile).reshape(1, tn)
            p, dt_t = project_even(x2, mod, i, seq, None, g, w_main, wdt_t, qg_t, kg_t, cols, tm=tm, tn=tn)
            pc, dtc_t = project_even(ctx2, mod, i, nctx, ctx_row, g, w_main, wdt_t, qg_t, kg_t, cols,
                                     tm=tm, tn=tn)
            p3 = p.reshape(bsz, seq, -1)
            pc3 = pc.reshape(bsz, nctx, -1)
            cmat = expand_bias(na_rpb[e])
            ya = neighbourhood_attention(p3, pc3, cmat, cols)
            dtb8 = _per_group_rows(ssd_dt_bias[e])
            a8 = _per_group_rows(-jnp.exp(ssd_a_log[e].astype(F32)))
            dskip = jnp.repeat(ssd_d[e].astype(F32), SSD_HEAD_DIM).reshape(1, -1)
            sng = ssd_norm_g[e].reshape(1, -1)
            cw = ssd_conv_w[e]
            cb = ssd_conv_b[e].reshape(1, -1)
            h_zero = jnp.zeros((bsz, SSD_GROUPS, 2, SSD_STATE, hp), F32)
            ybc, h_ctx = ssd_mixer(pc3, dtc_t, cw, cb, dtb8, a8, dskip, sng, h_zero, cols)
            yb, _ = ssd_mixer(p3, dt_t, cw, cb, dtb8, a8, dskip, sng, h_ctx, cols)
            x2_new = out_project([ya.reshape(bsz * seq, -1), yb.reshape(bsz * seq, -1)], na_ssd_w_out, e,
                                 x2, mod, i, seq, None, tm=tm_out_even, tn=tn)
            if update_ctx:
                yac = context_attention(pc3, cols)
                ctx2 = out_project([yac.reshape(bsz * nctx, -1), ybc.reshape(bsz * nctx, -1)],
                                   na_ssd_w_out, e, ctx2, mod, i, nctx, ctx_row, tm=tm_out_even, tn=tn)
            x2 = x2_new
        else:
            o = i // 2
            z = short_conv_mixer(x2, mod, i, seq, None, g, sc_w_in, o, sc_conv_w[o], tm=tm_sc, tc=tc)
            x2_new = out_project([z], sc_w_out, o, x2, mod, i, seq, None, tm=tm, tn=tn)
            if update_ctx:
                zc = short_conv_mixer(ctx2, mod, i, nctx, ctx_row, g, sc_w_in, o, sc_conv_w[o],
                                      tm=tm_sc, tc=tc)
                ctx2 = out_project([zc], sc_w_out, o, ctx2, mod, i, nctx, ctx_row, tm=tm, tn=tn)
            x2 = x2_new
    return x2.reshape(bsz, seq, d)


def kernel(x, c, ctx, c_ctx, ada_w, ada_b, norm_g, na_ssd_w_in, ssd_conv_w, ssd_conv_b, ssd_a_log,
           ssd_dt_bias, ssd_d, ssd_norm_g, q_norm_g, k_norm_g, na_rpb, na_ssd_w_out, sc_w_in,
           sc_conv_w, sc_w_out):
    return _forward(x, c, ctx, c_ctx, ada_w, ada_b, norm_g, na_ssd_w_in, ssd_conv_w, ssd_conv_b,
                    ssd_a_log, ssd_dt_bias, ssd_d, ssd_norm_g, q_norm_g, k_norm_g, na_rpb,
                    na_ssd_w_out, sc_w_in, sc_conv_w, sc_w_out, tm=2048, tn=512, tm_sc=1024, tc=256, tm_out_even=1024)
```

```python
import functools

import jax
import jax.numpy as jnp
import numpy as np
from jax import lax
from jax.experimental import pallas as pl
from jax.experimental.pallas import tpu as pltpu

F32 = jnp.float32
BF16 = jnp.bfloat16

EPS = 1e-6
GRID_W = 64
NA_HEADS = 16
NA_HEAD_DIM = 128
NA_KH = 8
NA_KW = 16
SSD_HEAD_DIM = 64
SSD_GROUPS = 8
SSD_HPG = 4
SSD_HEADS = SSD_GROUPS * SSD_HPG
SSD_STATE = 128
SSD_CONV = 5
SSD_CHUNK = 128
SC_CONV = 3

V7X_LANES = 128
V7X_VMEM_LIMIT = 56 * 1024 * 1024

MASKED = -1e30

NA_QROWS = 8
NA_KROWS = 16
_NA_SOFTMAX_ROWS = 32


def _silu(x):
    return x * (1.0 / (1.0 + jnp.exp(-x)))


def _softplus(x):
    return jnp.maximum(x, 0.0) + jnp.log(1.0 + jnp.exp(-jnp.abs(x)))


def _params(sem, vmem=V7X_VMEM_LIMIT):
    return pltpu.CompilerParams(dimension_semantics=sem, vmem_limit_bytes=vmem)


def _adaln_kernel(cond_ref, w_ref, b_ref, o_ref):
    s = _silu(cond_ref[...])
    w = w_ref[...]
    rows = s.shape[0]
    s_hi = s.astype(BF16).astype(F32)
    s_terms = jnp.concatenate([s_hi, s - s_hi], axis=0).astype(BF16)
    w_hi = w.astype(BF16)
    w_mid = (w - w_hi.astype(F32)).astype(BF16)
    both = jnp.dot(s_terms, w_hi, preferred_element_type=F32)
    acc = both[0:rows] + both[rows:2 * rows]
    acc += jnp.dot(s_hi.astype(BF16), w_mid, preferred_element_type=F32)
    o_ref[...] = acc + b_ref[...]


def adaln_all(cond, ada_w, ada_b, tn=768):
    depth, d, n = ada_w.shape
    rows = cond.shape[0]
    return pl.pallas_call(
        _adaln_kernel,
        grid=(depth, n // tn),
        in_specs=[
            pl.BlockSpec((rows, d), lambda l, j: (0, 0)),
            pl.BlockSpec((None, d, tn), lambda l, j: (l, 0, j)),
            pl.BlockSpec((None, 1, tn), lambda l, j: (l, 0, j)),
        ],
        out_specs=pl.BlockSpec((None, rows, tn), lambda l, j: (l, 0, j)),
        out_shape=jax.ShapeDtypeStruct((depth, rows, n), F32),
        compiler_params=_params(("parallel", "parallel")),
        name="adaln",
    )(cond, ada_w, ada_b.reshape(depth, 1, n))


_PROJ_ROWS = 128


_HEAD_NORM_ROWS = 256


def _proj_kernel(x_ref, shift_ref, scale_ref, g_ref, w_ref, wdt_ref, qg_ref, kg_ref, o_ref, odt_ref, h_scr,
                 *, q_tiles, k_tiles):
    tm = x_ref.shape[0]
    tn = o_ref.shape[1]
    j = pl.program_id(1)

    @pl.when(j == 0)
    def _():
        gain = g_ref[...] * (1.0 + scale_ref[...])
        shift = shift_ref[...]

        def body(r, carry):
            rows = pl.ds(pl.multiple_of(r * _PROJ_ROWS, _PROJ_ROWS), _PROJ_ROWS)
            x = x_ref[rows, :]
            ms = jnp.mean(x * x, axis=-1, keepdims=True)
            h_scr[rows, :] = (x * lax.rsqrt(ms + EPS) * gain + shift).astype(BF16)
            return carry

        lax.fori_loop(0, tm // _PROJ_ROWS, body, 0)
        odt_ref[...] = lax.dot_general(wdt_ref[...], h_scr[...], (((1,), (1,)), ((), ())),
                                       preferred_element_type=F32)

    is_q = (j >= q_tiles[0]) & (j < q_tiles[1])
    is_k = (j >= k_tiles[0]) & (j < k_tiles[1])

    @pl.when(is_q | is_k)
    def _():
        acc = jnp.dot(h_scr[...], w_ref[...], preferred_element_type=F32)
        gain = jnp.where(is_q, qg_ref[...], kg_ref[...])
        step = min(_HEAD_NORM_ROWS, tm)
        for r in range(tm // step):
            for hb in range(tn // NA_HEAD_DIM):
                lanes = slice(hb * NA_HEAD_DIM, (hb + 1) * NA_HEAD_DIM)
                a = acc[r * step:(r + 1) * step, lanes]
                ms = jnp.mean(a * a, axis=-1, keepdims=True)
                o_ref[r * step:(r + 1) * step, lanes] = (a * lax.rsqrt(ms + EPS) * gain[:, lanes]).astype(o_ref.dtype)

    @pl.when(jnp.logical_not(is_q | is_k))
    def _():
        o_ref[...] = jnp.dot(h_scr[...], w_ref[...], preferred_element_type=F32).astype(o_ref.dtype)


def project_even(x2, mod, layer, rows_per_seq, mod_row, g, w_stack, w_idx, wdt_t, qg_t, kg_t, cols, *, tm, tn):
    m, d = x2.shape
    n = cols["dt"]
    tm = min(tm, rows_per_seq if mod_row is None else m)
    tiles_per_seq = rows_per_seq // tm if mod_row is None else None
    row = (lambda i: i // tiles_per_seq) if mod_row is None else (lambda i: mod_row)
    na_width = NA_HEADS * NA_HEAD_DIM
    q_tiles = (cols["q"] // tn, (cols["q"] + na_width) // tn)
    k_tiles = (cols["k"] // tn, (cols["k"] + na_width) // tn)
    return pl.pallas_call(
        functools.partial(_proj_kernel, q_tiles=q_tiles, k_tiles=k_tiles),
        grid=(m // tm, n // tn),
        in_specs=[
            pl.BlockSpec((tm, d), lambda i, j: (i, 0)),
            pl.BlockSpec((None, None, 1, d), lambda i, j: (layer, row(i), 0, 0)),
            pl.BlockSpec((None, None, 1, d), lambda i, j: (layer, row(i), 0, 1)),
            pl.BlockSpec((1, d), lambda i, j: (0, 0)),
            pl.BlockSpec((None, d, tn), lambda i, j: (w_idx, 0, j)),
            pl.BlockSpec((wdt_t.shape[0], d), lambda i, j: (0, 0)),
            pl.BlockSpec((1, tn), lambda i, j: (0, 0)),
            pl.BlockSpec((1, tn), lambda i, j: (0, 0)),
        ],
        out_specs=[pl.BlockSpec((tm, tn), lambda i, j: (i, j)),
                   pl.BlockSpec((wdt_t.shape[0], tm), lambda i, j: (0, i))],
        out_shape=[jax.ShapeDtypeStruct((m, n), BF16),
                   jax.ShapeDtypeStruct((wdt_t.shape[0], m), F32)],
        scratch_shapes=[pltpu.VMEM((tm, d), BF16)],
        compiler_params=_params(("parallel", "arbitrary")),
        name="proj_dt",
    )(x2, mod, mod, g, w_stack, wdt_t, qg_t, kg_t)


def _bias_kernel(rpb_ref, o_ref):
    h = pl.program_id(0)
    qc = lax.broadcasted_iota(jnp.int32, (GRID_W, GRID_W), 0)
    kc = lax.broadcasted_iota(jnp.int32, (GRID_W, GRID_W), 1)
    start = jnp.clip(qc - NA_KW // 2, 0, GRID_W - NA_KW)
    in_win = (kc >= start) & (kc < start + NA_KW)
    dc = jnp.clip(kc - qc, -(NA_KW - 1), NA_KW - 1) + NA_KW - 1
    for dr in range(2 * NA_KH - 1):
        acc = jnp.zeros((GRID_W, GRID_W), F32)
        for j in range(2 * NA_KW - 1):
            acc = jnp.where(dc == j, rpb_ref[h, dr * (2 * NA_KW - 1) + j], acc)
        o_ref[dr] = jnp.where(in_win, acc, MASKED)


def expand_bias(rpb):
    nh = rpb.shape[0]
    ndr, ndc = 2 * NA_KH - 1, 2 * NA_KW - 1
    return pl.pallas_call(
        _bias_kernel,
        grid=(nh,),
        in_specs=[pl.BlockSpec(memory_space=pltpu.SMEM)],
        out_specs=pl.BlockSpec((None, ndr, GRID_W, GRID_W), lambda h: (h, 0, 0, 0)),
        out_shape=jax.ShapeDtypeStruct((nh, ndr, GRID_W, GRID_W), F32),
        compiler_params=_params(("arbitrary",)),
        name="rpb_expand",
    )(rpb.reshape(nh, ndr * ndc))


def _window_rows(case, qr, n_rows):
    if case == 1:
        lo, dr0 = qr, NA_KH - 1 - NA_KH // 2
    elif case == 0:
        lo = max(qr - NA_KH // 2, 0)
        dr0 = lo - qr + NA_KH - 1
    else:
        r = n_rows - NA_QROWS + qr
        rs = min(r - NA_KH // 2, n_rows - NA_KH)
        lo = rs - (n_rows - NA_KROWS)
        dr0 = rs - r + NA_KH - 1
    return lo, dr0


def _na_kernel(q_ref, gate_ref, k_ref, v_ref, kc_ref, vc_ref, cmat_ref, o_ref,
               v1_scr, vc1_scr, bias_scr, s0_scr, s1_scr, p0_scr, p1_scr, *, n_rows):
    seq = k_ref.shape[0]
    nctx = kc_ref.shape[0]
    dh = NA_HEAD_DIM
    nq = NA_QROWS * GRID_W
    nk = NA_KROWS * GRID_W
    nb = n_rows // NA_QROWS

    @pl.when(pl.program_id(1) == 0)
    def _():
        masked = jnp.full((GRID_W, GRID_W), MASKED, F32)
        for case in range(3):
            for qr in range(NA_QROWS):
                lo, dr0 = _window_rows(case, qr, n_rows)
                strip = [cmat_ref[dr0 + kr - lo] if lo <= kr < lo + NA_KH else masked
                         for kr in range(NA_KROWS)]
                bias_scr[case, qr * GRID_W:(qr + 1) * GRID_W, :] = jnp.concatenate(strip, axis=1)

    @pl.when((pl.program_id(0) == 0) & (pl.program_id(1) == 0))
    def _():
        v1_scr[:, dh:2 * dh] = jnp.ones((seq, dh), BF16)
        vc1_scr[:, dh:2 * dh] = jnp.ones((nctx, dh), BF16)

    v1_scr[:, 0:dh] = v_ref[...]
    vc1_scr[:, 0:dh] = vc_ref[...]

    nt = (((1,), (1,)), ((), ()))

    def block_rows(i):
        qrows = pl.ds(pl.multiple_of(i * nq, nq), nq)
        kstart = jnp.clip(i * NA_QROWS - NA_KH // 2, 0, n_rows - NA_KROWS)
        krows = pl.ds(pl.multiple_of(kstart * GRID_W, GRID_W), nk)
        return qrows, krows

    def scores(i, s_scr):
        i = jnp.minimum(i, nb - 1)
        qrows, krows = block_rows(i)
        qn = q_ref[qrows, :]
        case = jnp.where(i == 0, 0, jnp.where(i == nb - 1, 2, 1))
        s_scr[:, 0:nctx] = lax.dot_general(qn, kc_ref[...], nt, preferred_element_type=F32)
        s_scr[:, nctx:nctx + nk] = (lax.dot_general(qn, k_ref[krows, :], nt, preferred_element_type=F32)
                                    + bias_scr[case])

    def softmax(s_scr, p_scr):
        for g in range(nq // _NA_SOFTMAX_ROWS):
            rows = slice(g * _NA_SOFTMAX_ROWS, (g + 1) * _NA_SOFTMAX_ROWS)
            s = s_scr[rows, :]
            p_scr[rows, :] = jnp.exp(s - jnp.max(s, axis=-1, keepdims=True)).astype(BF16)

    def values(i, p_scr):
        qrows, krows = block_rows(i)
        o = jnp.dot(p_scr[:, 0:nctx], vc1_scr[...], preferred_element_type=F32)
        o += jnp.dot(p_scr[:, nctx:nctx + nk], v1_scr[krows, :], preferred_element_type=F32)
        gated = o[:, 0:dh] * (1.0 / o[:, dh:2 * dh]) * _silu(gate_ref[qrows, :].astype(F32))
        o_ref[qrows, :] = gated.astype(o_ref.dtype)

    scores(0, s0_scr)
    softmax(s0_scr, p0_scr)
    scores(1, s1_scr)

    def pair(jj, carry):
        j = 2 * jj
        values(j, p0_scr)
        softmax(s1_scr, p1_scr)
        scores(j + 2, s0_scr)
        values(j + 1, p1_scr)
        softmax(s0_scr, p0_scr)
        scores(j + 3, s1_scr)
        return carry

    lax.fori_loop(0, nb // 2, pair, 0)


def neighbourhood_attention(p, pc, cmat, cols):
    bsz, seq, _ = p.shape
    nctx = pc.shape[1]
    n_rows = seq // GRID_W
    nb = n_rows // NA_QROWS
    assert n_rows >= NA_KROWS and n_rows % NA_QROWS == 0 and nb % 2 == 0 and nctx % V7X_LANES == 0
    nq = NA_QROWS * GRID_W
    nk = NA_KROWS * GRID_W
    dh = NA_HEAD_DIM
    cq, cg, ck, cv = (cols[k] // dh for k in ("q", "gate", "k", "v"))
    slab = lambda c: pl.BlockSpec((None, seq, dh), lambda h, b: (b, 0, c + h))
    cslab = lambda c: pl.BlockSpec((None, nctx, dh), lambda h, b: (b, 0, c + h))
    return pl.pallas_call(
        functools.partial(_na_kernel, n_rows=n_rows),
        grid=(NA_HEADS, bsz),
        in_specs=[
            slab(cq), slab(cg), slab(ck), slab(cv), cslab(ck), cslab(cv),
            pl.BlockSpec((None, 2 * NA_KH - 1, GRID_W, GRID_W), lambda h, b: (h, 0, 0, 0)),
        ],
        out_specs=pl.BlockSpec((None, seq, dh), lambda h, b: (b, 0, h)),
        out_shape=jax.ShapeDtypeStruct((bsz, seq, NA_HEADS * dh), BF16),
        scratch_shapes=[
            pltpu.VMEM((seq, 2 * dh), BF16),
            pltpu.VMEM((nctx, 2 * dh), BF16),
            pltpu.VMEM((3, nq, nk), F32),
            pltpu.VMEM((nq, nctx + nk), F32),
            pltpu.VMEM((nq, nctx + nk), F32),
            pltpu.VMEM((nq, nctx + nk), BF16),
            pltpu.VMEM((nq, nctx + nk), BF16),
        ],
        compiler_params=_params(("arbitrary", "arbitrary")),
        name="nbr_attention",
    )(p, p, p, p, pc, pc, cmat)


def _ctx_attn_kernel(q_ref, gate_ref, k_ref, v_ref, o_ref):
    s = lax.dot_general(q_ref[...], k_ref[...], (((1,), (1,)), ((), ())), preferred_element_type=F32)
    m = jnp.max(s, axis=-1, keepdims=True)
    pr = jnp.exp(s - m)
    l = jnp.sum(pr, axis=-1, keepdims=True)
    o = jnp.dot(pr.astype(BF16), v_ref[...], preferred_element_type=F32)
    o_ref[...] = (o * (1.0 / l) * _silu(gate_ref[...].astype(F32))).astype(o_ref.dtype)


def context_attention(pc, cols):
    bsz, nctx, _ = pc.shape
    dh = NA_HEAD_DIM
    cq, cg, ck, cv = (cols[k] // dh for k in ("q", "gate", "k", "v"))
    blk = lambda c: pl.BlockSpec((None, nctx, dh), lambda b, h: (b, 0, c + h))
    return pl.pallas_call(
        _ctx_attn_kernel,
        grid=(bsz, NA_HEADS),
        in_specs=[blk(cq), blk(cg), blk(ck), blk(cv)],
        out_specs=pl.BlockSpec((None, nctx, dh), lambda b, h: (b, 0, h)),
        out_shape=jax.ShapeDtypeStruct((bsz, nctx, NA_HEADS * dh), BF16),
        compiler_params=_params(("parallel", "parallel")),
        name="ctx_attention",
    )(pc, pc, pc, pc)


_CONV_HALO = 16


def _conv_silu_chunk(src_ref, c, nc, w_ref, b_ref, stage_ref):
    t = SSD_CHUNK
    hl = _CONV_HALO
    length = src_ref.shape[0]
    start = pl.multiple_of(c * t, t)
    prev = src_ref[pl.ds(pl.multiple_of(jnp.maximum(start - hl, 0), hl), hl), :].astype(F32)
    nxt = src_ref[pl.ds(pl.multiple_of(jnp.minimum(start + t, length - hl), hl), hl), :].astype(F32)
    stage_ref[0:hl, :] = jnp.where(c > 0, prev, 0.0)
    stage_ref[hl:hl + t, :] = src_ref[pl.ds(start, t), :].astype(F32)
    stage_ref[hl + t:hl + t + hl, :] = jnp.where(c < nc - 1, nxt, 0.0)
    k = w_ref.shape[0]
    acc = jnp.zeros((t, src_ref.shape[1]), F32) + b_ref[...]
    for j in range(k):
        off = hl - k // 2 + j
        acc += stage_ref[off:off + t, :] * w_ref[j:j + 1, :]
    return _silu(acc)


def _ssd_kernel(x_ref, bm_ref, cm_ref, z_ref, dtt_ref, wx_ref, bx_ref, wb_ref, bb_ref, wc_ref, bc_ref,
                dtb_ref, a_ref, dskip_ref, ng_ref, h0_ref, y_ref, hfin_ref,
                xs_scr, bmc_scr, cmc_scr, yacc_scr, stx_scr, stb_scr, rows_scr, col_scr, fac_scr):
    t = SSD_CHUNK
    length = x_ref.shape[0]
    nc = length // t
    hp = SSD_HPG * SSD_HEAD_DIM

    def conv_chunk(c, side):
        rows = pl.ds(pl.multiple_of(c * t, t), t)
        xs_scr[rows, :] = _conv_silu_chunk(x_ref, c, nc, wx_ref, bx_ref, stx_scr.at[side]).astype(BF16)
        bmc_scr[rows, :] = _conv_silu_chunk(bm_ref, c, nc, wb_ref, bb_ref, stb_scr.at[2 * side]).astype(BF16)
        cmc_scr[rows, :] = _conv_silu_chunk(cm_ref, c, nc, wc_ref, bc_ref,
                                            stb_scr.at[2 * side + 1]).astype(BF16)

    conv_chunk(0, 0)
    conv_chunk(nc - 1, 1)
    hfin_ref[...] = h0_ref[...]

    ii = lax.broadcasted_iota(jnp.int32, (t, t), 0)
    jj = lax.broadcasted_iota(jnp.int32, (t, t), 1)
    lower = ii >= jj
    diag = ii == jj
    indicator = lambda cond: jnp.where(cond, 1.0, 0.0).astype(BF16)
    tri2_b = jnp.concatenate([indicator(ii <= jj), indicator(lower)], axis=1)
    fwd_rows = lax.broadcasted_iota(jnp.int32, (2 * SSD_HPG, t), 0) < SSD_HPG
    er = lax.broadcasted_iota(jnp.int32, (2 * SSD_HPG, hp), 0)
    el = lax.broadcasted_iota(jnp.int32, (2 * SSD_HPG, hp), 1) // SSD_HEAD_DIM
    expand = [indicator(er == el + d * SSD_HPG) for d in range(2)]
    lane_head = lax.broadcasted_iota(jnp.int32, (t, hp), 1) // SSD_HEAD_DIM
    head_lanes = [indicator(lane_head == r) for r in range(SSD_HPG)]
    nt = (((1,), (1,)), ((), ()))
    tn = (((0,), (0,)), ((), ()))

    def chunk_rows(c):
        return pl.ds(pl.multiple_of(c * t, t), t)

    def decay_terms(c):
        dt8 = _softplus(dtt_ref[:, chunk_rows(c)] + dtb_ref[...])
        dta8 = dt8 * a_ref[...]
        hi = dta8.astype(BF16).astype(F32)
        rest = dta8 - hi
        mid = rest.astype(BF16).astype(F32)
        terms = jnp.concatenate([hi, mid, rest - mid], axis=0).astype(BF16)
        cs3 = jnp.dot(terms, tri2_b, preferred_element_type=F32)
        cs2 = cs3[0:8] + cs3[8:16] + cs3[16:24]
        cs_row = jnp.where(fwd_rows, cs2[:, 0:t], cs2[:, t:2 * t])
        return dt8, cs_row, dt8.T, cs_row.T

    def factors(d, dt_col, cs_col):
        edge = t - 1 if d == 0 else 0
        to_end_col = jnp.exp(jnp.minimum(cs_col[edge:edge + 1, :] - cs_col, 0.0)) * dt_col
        ecs = jnp.dot(jnp.exp(cs_col).astype(BF16), expand[d], preferred_element_type=F32)
        to_end = jnp.dot(to_end_col.astype(BF16), expand[d], preferred_element_type=F32)
        return ecs, to_end

    def prepare(k, buf):
        dt8, cs_row, dt_col, cs_col = decay_terms(jnp.minimum(k, nc - 1))
        rows_scr[buf, 0:8, :] = dt8
        rows_scr[buf, 8:16, :] = cs_row
        col_scr[buf] = cs_col
        ecs, to_end = factors(0, dt_col, cs_col)
        fac_scr[buf, 0] = ecs
        fac_scr[buf, 1] = to_end
        _, _, dt_col_b, cs_col_b = decay_terms(jnp.maximum(nc - 1 - k, 0))
        ecs_b, to_end_b = factors(1, dt_col_b, cs_col_b)
        fac_scr[buf, 2] = ecs_b
        fac_scr[buf, 3] = to_end_b

    def intra(c, buf):
        dt8 = rows_scr[buf, 0:8, :]
        cs_row = rows_scr[buf, 8:16, :]
        cs_col = col_scr[buf]
        rows = chunk_rows(c)
        xs = xs_scr[rows, :]
        cb = lax.dot_general(cmc_scr[rows, :], bmc_scr[rows, :], nt, preferred_element_type=F32)
        cb_diag = jnp.where(diag, cb, 0.0)
        ws, xbd = [], []
        for r in range(SSD_HPG):
            f, b = r, SSD_HPG + r
            seg = jnp.where(lower, cs_col[:, f:f + 1] - cs_row[f:f + 1, :],
                            cs_col[:, b:b + 1] - cs_row[b:b + 1, :])
            dt_sel = jnp.where(lower, dt8[f:f + 1, :], dt8[b:b + 1, :])
            w = cb * (jnp.exp(seg) * dt_sel) + cb_diag * dt8[b:b + 1, :]
            ws.append(w.astype(BF16))
            xbd.append(xs * head_lanes[r])
        return jnp.dot(jnp.concatenate(ws, axis=1), jnp.concatenate(xbd, axis=0), preferred_element_type=F32)

    def state(c, d, buf):
        rows = chunk_rows(c)
        edge = t - 1 if d == 0 else 0
        ecs = fac_scr[buf, 2 * d]
        to_end = fac_scr[buf, 2 * d + 1]
        xs = xs_scr[rows, :]
        h = hfin_ref[d]
        y = jnp.dot(cmc_scr[rows, :], h.astype(BF16), preferred_element_type=F32) * ecs
        xw = (xs.astype(F32) * to_end).astype(BF16)
        hfin_ref[d] = h * ecs[edge:edge + 1, :] + lax.dot_general(bmc_scr[rows, :], xw, tn,
                                                                  preferred_element_type=F32)
        return y

    def finish(c, y):
        rows = chunk_rows(c)
        y = y + yacc_scr[rows, :] + dskip_ref[...] * xs_scr[rows, :].astype(F32)
        yz = y * _silu(z_ref[rows, :].astype(F32))
        yz = yz * lax.rsqrt(jnp.mean(yz * yz, axis=-1, keepdims=True) + EPS)
        y_ref[rows, :] = (yz * ng_ref[...]).astype(y_ref.dtype)

    def first_half(k, buf):
        yacc_scr[chunk_rows(k), :] = intra(k, buf) + state(k, 0, buf)
        yacc_scr[chunk_rows(nc - 1 - k), :] = state(nc - 1 - k, 1, buf)
        prepare(k + 1, 1 - buf)
        conv_chunk(k + 1, 0)
        conv_chunk(nc - 2 - k, 1)

    def second_half(k, buf):
        finish(k, intra(k, buf) + state(k, 0, buf))
        finish(nc - 1 - k, state(nc - 1 - k, 1, buf))
        prepare(k + 1, 1 - buf)

    def walk(body, start, stop):
        if start % 2 == 0 and (stop - start) % 2 == 0 and stop - start > 2:
            def pair(kk, carry):
                body(2 * kk, 0)
                body(2 * kk + 1, 1)
                return carry

            lax.fori_loop(start // 2, stop // 2, pair, 0)
        else:
            for k in range(start, stop):
                body(jnp.int32(k), k % 2)

    prepare(0, 0)
    walk(first_half, 0, nc // 2)
    walk(second_half, nc // 2, nc)


def ssd_mixer(p, dt_t, conv_w, conv_b, dtb8, a8, dskip, norm_g, h0, cols):
    bsz, length, _ = p.shape
    hp = SSD_HPG * SSD_HEAD_DIM
    ns = SSD_STATE
    width = SSD_GROUPS * hp
    k = conv_w.shape[0]
    cx = cols["xbc"] // hp
    cb_ = (cols["xbc"] + width) // ns
    cc_ = (cols["xbc"] + width + SSD_GROUPS * ns) // ns
    cz = cols["z"] // hp
    st_rows = SSD_CHUNK + 2 * _CONV_HALO
    return pl.pallas_call(
        _ssd_kernel,
        grid=(bsz, SSD_GROUPS),
        in_specs=[
            pl.BlockSpec((None, length, hp), lambda b, g: (b, 0, cx + g)),
            pl.BlockSpec((None, length, ns), lambda b, g: (b, 0, cb_ + g)),
            pl.BlockSpec((None, length, ns), lambda b, g: (b, 0, cc_ + g)),
            pl.BlockSpec((None, length, hp), lambda b, g: (b, 0, cz + g)),
            pl.BlockSpec((2 * SSD_HPG, length), lambda b, g: (g, b)),
            pl.BlockSpec((k, hp), lambda b, g: (0, g)),
            pl.BlockSpec((1, hp), lambda b, g: (0, g)),
            pl.BlockSpec((k, ns), lambda b, g: (0, width // ns + g)),
            pl.BlockSpec((1, ns), lambda b, g: (0, width // ns + g)),
            pl.BlockSpec((k, ns), lambda b, g: (0, width // ns + SSD_GROUPS + g)),
            pl.BlockSpec((1, ns), lambda b, g: (0, width // ns + SSD_GROUPS + g)),
            pl.BlockSpec((None, 2 * SSD_HPG, V7X_LANES), lambda b, g: (g, 0, 0)),
            pl.BlockSpec((None, 2 * SSD_HPG, V7X_LANES), lambda b, g: (g, 0, 0)),
            pl.BlockSpec((1, hp), lambda b, g: (0, g)),
            pl.BlockSpec((1, hp), lambda b, g: (0, g)),
            pl.BlockSpec((None, None, 2, ns, hp), lambda b, g: (b, g, 0, 0, 0)),
        ],
        out_specs=[
            pl.BlockSpec((None, length, hp), lambda b, g: (b, 0, g)),
            pl.BlockSpec((None, None, 2, ns, hp), lambda b, g: (b, g, 0, 0, 0)),
        ],
        out_shape=[
            jax.ShapeDtypeStruct((bsz, length, width), BF16),
            jax.ShapeDtypeStruct((bsz, SSD_GROUPS, 2, ns, hp), F32),
        ],
        scratch_shapes=[
            pltpu.VMEM((length, hp), BF16),
            pltpu.VMEM((length, ns), BF16),
            pltpu.VMEM((length, ns), BF16),
            pltpu.VMEM((length, hp), F32),
            pltpu.VMEM((2, st_rows, hp), F32),
            pltpu.VMEM((4, st_rows, ns), F32),
            pltpu.VMEM((2, 4 * SSD_HPG, SSD_CHUNK), F32),
            pltpu.VMEM((2, SSD_CHUNK, 2 * SSD_HPG), F32),
            pltpu.VMEM((2, 4, SSD_CHUNK, hp), F32),
        ],
        compiler_params=_params(("parallel", "parallel")),
        name="ssd_mixer",
    )(p, p, p, p, dt_t, conv_w, conv_b, conv_w, conv_b, conv_w, conv_b, dtb8, a8, dskip, norm_g, h0)


def _out_kernel(*refs, n_in):
    a_refs, w_refs = refs[:n_in], refs[n_in:2 * n_in]
    x_ref, gate_ref, o_ref = refs[2 * n_in:]
    acc = jnp.dot(a_refs[0][...], w_refs[0][...].astype(BF16), preferred_element_type=F32)
    for a_ref, w_ref in zip(a_refs[1:], w_refs[1:]):
        acc += jnp.dot(a_ref[...], w_ref[...].astype(BF16), preferred_element_type=F32)
    o_ref[...] = x_ref[...] + gate_ref[...] * acc


def out_project(acts, w_stack, w_idx, x2, mod, layer, rows_per_seq, mod_row, *, tm, tn):
    m, d = x2.shape
    kw = acts[0].shape[1]
    assert all(a.shape[1] == kw for a in acts) and w_stack.shape[1] == kw * len(acts)
    w_out = w_stack
    tm = min(tm, rows_per_seq)
    tiles_per_seq = rows_per_seq // tm
    row = (lambda i: i // tiles_per_seq) if mod_row is None else (lambda i: mod_row)
    w_spec = lambda n: pl.BlockSpec((None, kw, tn), lambda i, j: (w_idx, n, j))
    return pl.pallas_call(
        functools.partial(_out_kernel, n_in=len(acts)),
        grid=(m // tm, d // tn),
        in_specs=[pl.BlockSpec((tm, kw), lambda i, j: (i, 0)) for _ in acts]
        + [w_spec(n) for n in range(len(acts))]
        + [pl.BlockSpec((tm, tn), lambda i, j: (i, j)),
           pl.BlockSpec((None, None, 1, tn), lambda i, j: (layer, row(i), 0, 2 * (d // tn) + j))],
        out_specs=pl.BlockSpec((tm, tn), lambda i, j: (i, j)),
        out_shape=jax.ShapeDtypeStruct((m, d), F32),
        compiler_params=_params(("parallel", "arbitrary")),
        name="out_proj",
    )(*acts, *([w_out] * len(acts)), x2, mod)


_SC_HALO = 16


def _short_conv_kernel(x_ref, xp_ref, xn_ref, shift_ref, scale_ref, g_ref, wb_ref, wc_ref, wh_ref, wg_ref,
                       cw_ref, z_ref, h_scr, u_scr, *, tiles_per_seq):
    tm = x_ref.shape[0]
    hl = _SC_HALO

    @pl.when(pl.program_id(1) == 0)
    def _():
        gain = g_ref[...] * (1.0 + scale_ref[...])
        shift = shift_ref[...]

        def norm(x):
            ms = jnp.mean(x * x, axis=-1, keepdims=True)
            return (x * lax.rsqrt(ms + EPS) * gain + shift).astype(BF16)

        h_scr[0:hl, :] = norm(xp_ref[...])
        h_scr[hl + tm:hl + tm + hl, :] = norm(xn_ref[...])

        def body(r, carry):
            rows = pl.multiple_of(r * _PROJ_ROWS, _PROJ_ROWS)
            h_scr[pl.ds(hl + rows, _PROJ_ROWS), :] = norm(x_ref[pl.ds(rows, _PROJ_ROWS), :])
            return carry

        lax.fori_loop(0, tm // _PROJ_ROWS, body, 0)

    t_in_seq = pl.program_id(0) % tiles_per_seq
    hs = h_scr[...]
    u = (jnp.dot(hs, wc_ref[...].astype(BF16), preferred_element_type=F32)
         * jnp.dot(hs, wh_ref[...].astype(BF16), preferred_element_type=F32))
    u_scr[0:hl, :] = jnp.where(t_in_seq > 0, u[0:hl], 0.0)
    u_scr[hl:hl + tm, :] = u[hl:hl + tm]
    u_scr[hl + tm:hl + tm + hl, :] = jnp.where(t_in_seq < tiles_per_seq - 1, u[hl + tm:], 0.0)
    k = cw_ref.shape[0]
    acc = u_scr[hl - k // 2:hl - k // 2 + tm, :] * cw_ref[0:1, :]
    for j in range(1, k):
        off = hl - k // 2 + j
        acc += u_scr[off:off + tm, :] * cw_ref[j:j + 1, :]
    hm = h_scr[hl:hl + tm, :]
    bg = jnp.dot(hm, wb_ref[...].astype(BF16), preferred_element_type=F32)
    gt = jnp.dot(hm, wg_ref[...].astype(BF16), preferred_element_type=F32)
    z_ref[...] = (_silu(gt) * bg * acc).astype(z_ref.dtype)


def short_conv_mixer(x2, mod, layer, rows_per_seq, mod_row, g, w_stack, w_idx, conv_w, *, tm, tc):
    m, d = x2.shape
    w_in = w_stack
    w = w_in.shape[2] // 4
    tm = min(tm, rows_per_seq)
    tiles_per_seq = rows_per_seq // tm
    row = (lambda i: i // tiles_per_seq) if mod_row is None else (lambda i: mod_row)
    hb = tm // _SC_HALO
    last = m // _SC_HALO - 1
    nblk = w // tc
    w_spec = lambda part: pl.BlockSpec((None, d, tc), lambda i, j: (w_idx, 0, part * nblk + j))
    return pl.pallas_call(
        functools.partial(_short_conv_kernel, tiles_per_seq=tiles_per_seq),
        grid=(m // tm, nblk),
        in_specs=[
            pl.BlockSpec((tm, d), lambda i, j: (i, 0)),
            pl.BlockSpec((_SC_HALO, d), lambda i, j: (jnp.maximum(i * hb - 1, 0), 0)),
            pl.BlockSpec((_SC_HALO, d), lambda i, j: (jnp.minimum((i + 1) * hb, last), 0)),
            pl.BlockSpec((None, None, 1, d), lambda i, j: (layer, row(i), 0, 0)),
            pl.BlockSpec((None, None, 1, d), lambda i, j: (layer, row(i), 0, 1)),
            pl.BlockSpec((1, d), lambda i, j: (0, 0)),
            w_spec(0), w_spec(1), w_spec(2), w_spec(3),
            pl.BlockSpec((conv_w.shape[0], tc), lambda i, j: (0, j)),
        ],
        out_specs=pl.BlockSpec((tm, tc), lambda i, j: (i, j)),
        out_shape=jax.ShapeDtypeStruct((m, w), BF16),
        scratch_shapes=[pltpu.VMEM((tm + 2 * _SC_HALO, d), BF16),
                        pltpu.VMEM((tm + 2 * _SC_HALO, tc), F32)],
        compiler_params=_params(("parallel", "arbitrary")),
        name="short_conv",
    )(x2, x2, x2, mod, mod, g, w_in, w_in, w_in, w_in, conv_w)


def _even_columns(d_model):
    na_width = NA_HEADS * NA_HEAD_DIM
    ssd_width = SSD_GROUPS * SSD_HPG * SSD_HEAD_DIM
    cols = {"q": 0}
    cols["gate"] = cols["q"] + na_width
    cols["z"] = cols["gate"] + na_width
    cols["k"] = cols["z"] + ssd_width
    cols["v"] = cols["k"] + na_width
    cols["xbc"] = cols["v"] + na_width
    cols["dt"] = cols["xbc"] + ssd_width + 2 * SSD_GROUPS * SSD_STATE
    return cols


def _dt_weight_rows(w_in, cols):
    wdt = w_in[:, cols["dt"]:cols["dt"] + 2 * SSD_HEADS]
    wdt = wdt.reshape(-1, 2, SSD_GROUPS, SSD_HPG).transpose(2, 1, 3, 0).reshape(2 * SSD_HEADS, -1)
    pad = jnp.zeros((V7X_LANES - 2 * SSD_HEADS, wdt.shape[1]), wdt.dtype)
    return jnp.concatenate([wdt, pad], axis=0).astype(BF16)


def _per_group_rows(v):
    v = v.astype(F32).reshape(2, SSD_GROUPS, SSD_HPG).transpose(1, 0, 2).reshape(SSD_GROUPS, 2 * SSD_HPG, 1)
    return jnp.broadcast_to(v, (SSD_GROUPS, 2 * SSD_HPG, V7X_LANES))


def _forward(x, c, ctx, c_ctx, ada_w, ada_b, norm_g, na_ssd_w_in, ssd_conv_w, ssd_conv_b, ssd_a_log,
             ssd_dt_bias, ssd_d, ssd_norm_g, q_norm_g, k_norm_g, na_rpb, na_ssd_w_out, sc_w_in,
             sc_conv_w, sc_w_out, *, tm, tn, tm_sc, tc, tm_out_even):
    bsz, seq, d = x.shape
    nctx = ctx.shape[1]
    depth = ada_w.shape[0]
    cols = _even_columns(d)
    hp = SSD_HPG * SSD_HEAD_DIM

    cond = jnp.concatenate([c, c_ctx[None, :], jnp.zeros((8 - bsz - 1, d), F32)], axis=0)
    mod = adaln_all(cond, ada_w, ada_b).reshape(depth, 8, 1, 3 * d)
    ctx_row = bsz

    w_in_even = na_ssd_w_in.astype(BF16)
    w_out_even = na_ssd_w_out.astype(BF16)

    x2 = x.reshape(bsz * seq, d)
    ctx2 = ctx.reshape(bsz * nctx, d)
    for i in range(depth):
        update_ctx = any(j % 2 == 0 for j in range(i + 1, depth))
        g = norm_g[i].reshape(1, d)
        if i % 2 == 0:
            e = i // 2
            wdt_t = _dt_weight_rows(na_ssd_w_in[e], cols)
            heads_per_tile = tn // NA_HEAD_DIM
            qg_t = jnp.tile(q_norm_g[e].astype(F32) * NA_HEAD_DIM ** -0.5, heads_per_tile).reshape(1, tn)
            kg_t = jnp.tile(k_norm_g[e].astype(F32), heads_per_tile).reshape(1, tn)
            p, dt_t = project_even(x2, mod, i, seq, None, g, w_in_even, e, wdt_t, qg_t, kg_t, cols,
                                   tm=tm, tn=tn)
            pc, dtc_t = project_even(ctx2, mod, i, nctx, ctx_row, g, w_in_even, e, wdt_t, qg_t, kg_t, cols,
                                     tm=tm, tn=tn)
            p3 = p.reshape(bsz, seq, -1)
            pc3 = pc.reshape(bsz, nctx, -1)
            cmat = expand_bias(na_rpb[e])
            ya = neighbourhood_attention(p3, pc3, cmat, cols)
            dtb8 = _per_group_rows(ssd_dt_bias[e])
            a8 = _per_group_rows(-jnp.exp(ssd_a_log[e].astype(F32)))
            dskip = jnp.repeat(ssd_d[e].astype(F32), SSD_HEAD_DIM).reshape(1, -1)
            sng = ssd_norm_g[e].reshape(1, -1)
            cw = ssd_conv_w[e]
            cb = ssd_conv_b[e].reshape(1, -1)
            h_zero = jnp.zeros((bsz, SSD_GROUPS, 2, SSD_STATE, hp), F32)
            ybc, h_ctx = ssd_mixer(pc3, dtc_t, cw, cb, dtb8, a8, dskip, sng, h_zero, cols)
            yb, _ = ssd_mixer(p3, dt_t, cw, cb, dtb8, a8, dskip, sng, h_ctx, cols)
            x2_new = out_project([ya.reshape(bsz * seq, -1), yb.reshape(bsz * seq, -1)], w_out_even, e,
                                 x2, mod, i, seq, None, tm=tm_out_even, tn=tn)
            if update_ctx:
                yac = context_attention(pc3, cols)
                ctx2 = out_project([yac.reshape(bsz * nctx, -1), ybc.reshape(bsz * nctx, -1)],
                                   w_out_even, e, ctx2, mod, i, nctx, ctx_row, tm=tm_out_even, tn=tn)
            x2 = x2_new
        else:
            o = i // 2
            z = short_conv_mixer(x2, mod, i, seq, None, g, sc_w_in, o, sc_conv_w[o], tm=tm_sc, tc=tc)
            x2_new = out_project([z], sc_w_out, o, x2, mod, i, seq, None, tm=tm, tn=tn)
            if update_ctx:
                zc = short_conv_mixer(ctx2, mod, i, nctx, ctx_row, g, sc_w_in, o, sc_conv_w[o],
                                      tm=tm_sc, tc=tc)
                ctx2 = out_project([zc], sc_w_out, o, ctx2, mod, i, nctx, ctx_row, tm=tm, tn=tn)
            x2 = x2_new
    return x2.reshape(bsz, seq, d)


def kernel(x, c, ctx, c_ctx, ada_w, ada_b, norm_g, na_ssd_w_in, ssd_conv_w, ssd_conv_b, ssd_a_log,
           ssd_dt_bias, ssd_d, ssd_norm_g, q_norm_g, k_norm_g, na_rpb, na_ssd_w_out, sc_w_in,
           sc_conv_w, sc_w_out):
    return _forward(x, c, ctx, c_ctx, ada_w, ada_b, norm_g, na_ssd_w_in, ssd_conv_w, ssd_conv_b,
                    ssd_a_log, ssd_dt_bias, ssd_d, ssd_norm_g, q_norm_g, k_norm_g, na_rpb,
                    na_ssd_w_out, sc_w_in, sc_conv_w, sc_w_out, tm=2048, tn=512, tm_sc=1024, tc=256, tm_out_even=1024)
```

```python
import functools

import jax
import jax.numpy as jnp
import numpy as np
from jax import lax
from jax.experimental import pallas as pl
from jax.experimental.pallas import tpu as pltpu

F32 = jnp.float32
BF16 = jnp.bfloat16

EPS = 1e-6
GRID_W = 64
NA_HEADS = 16
NA_HEAD_DIM = 128
NA_KH = 8
NA_KW = 16
SSD_HEAD_DIM = 64
SSD_GROUPS = 8
SSD_HPG = 4
SSD_HEADS = SSD_GROUPS * SSD_HPG
SSD_STATE = 128
SSD_CONV = 5
SSD_CHUNK = 128
SC_CONV = 3

V7X_LANES = 128
V7X_VMEM_LIMIT = 56 * 1024 * 1024

MASKED = -1e30

NA_QROWS = 8
NA_KROWS = 16
_NA_SOFTMAX_ROWS = 32


def _silu(x):
    return x * (1.0 / (1.0 + jnp.exp(-x)))


def _softplus(x):
    return jnp.maximum(x, 0.0) + jnp.log(1.0 + jnp.exp(-jnp.abs(x)))


def _params(sem, vmem=V7X_VMEM_LIMIT):
    return pltpu.CompilerParams(dimension_semantics=sem, vmem_limit_bytes=vmem)


def _adaln_kernel(cond_ref, w_ref, b_ref, o_ref):
    s = _silu(cond_ref[...])
    w = w_ref[...]
    rows = s.shape[0]
    s_hi = s.astype(BF16).astype(F32)
    s_terms = jnp.concatenate([s_hi, s - s_hi], axis=0).astype(BF16)
    w_hi = w.astype(BF16)
    w_mid = (w - w_hi.astype(F32)).astype(BF16)
    both = jnp.dot(s_terms, w_hi, preferred_element_type=F32)
    acc = both[0:rows] + both[rows:2 * rows]
    acc += jnp.dot(s_hi.astype(BF16), w_mid, preferred_element_type=F32)
    o_ref[...] = acc + b_ref[...]


def adaln_all(cond, ada_w, ada_b, tn=768):
    depth, d, n = ada_w.shape
    rows = cond.shape[0]
    return pl.pallas_call(
        _adaln_kernel,
        grid=(depth, n // tn),
        in_specs=[
            pl.BlockSpec((rows, d), lambda l, j: (0, 0)),
            pl.BlockSpec((None, d, tn), lambda l, j: (l, 0, j)),
            pl.BlockSpec((None, 1, tn), lambda l, j: (l, 0, j)),
        ],
        out_specs=pl.BlockSpec((None, rows, tn), lambda l, j: (l, 0, j)),
        out_shape=jax.ShapeDtypeStruct((depth, rows, n), F32),
        compiler_params=_params(("parallel", "parallel")),
        name="adaln",
    )(cond, ada_w, ada_b.reshape(depth, 1, n))


_PROJ_ROWS = 128


_HEAD_NORM_ROWS = 256


def _proj_kernel(x_ref, shift_ref, scale_ref, g_ref, w_ref, wdt_ref, qg_ref, kg_ref, o_ref, odt_ref, h_scr,
                 *, q_tiles, k_tiles):
    tm = x_ref.shape[0]
    tn = o_ref.shape[1]
    j = pl.program_id(1)

    @pl.when(j == 0)
    def _():
        gain = g_ref[...] * (1.0 + scale_ref[...])
        shift = shift_ref[...]

        def body(r, carry):
            rows = pl.ds(pl.multiple_of(r * _PROJ_ROWS, _PROJ_ROWS), _PROJ_ROWS)
            x = x_ref[rows, :]
            ms = jnp.mean(x * x, axis=-1, keepdims=True)
            h_scr[rows, :] = (x * lax.rsqrt(ms + EPS) * gain + shift).astype(BF16)
            return carry

        lax.fori_loop(0, tm // _PROJ_ROWS, body, 0)
        odt_ref[...] = lax.dot_general(wdt_ref[...], h_scr[...], (((1,), (1,)), ((), ())),
                                       preferred_element_type=F32)

    is_q = (j >= q_tiles[0]) & (j < q_tiles[1])
    is_k = (j >= k_tiles[0]) & (j < k_tiles[1])

    @pl.when(is_q | is_k)
    def _():
        acc = jnp.dot(h_scr[...], w_ref[...], preferred_element_type=F32)
        gain = jnp.where(is_q, qg_ref[...], kg_ref[...])
        step = min(_HEAD_NORM_ROWS, tm)
        for r in range(tm // step):
            for hb in range(tn // NA_HEAD_DIM):
                lanes = slice(hb * NA_HEAD_DIM, (hb + 1) * NA_HEAD_DIM)
                a = acc[r * step:(r + 1) * step, lanes]
                ms = jnp.mean(a * a, axis=-1, keepdims=True)
                o_ref[r * step:(r + 1) * step, lanes] = (a * lax.rsqrt(ms + EPS) * gain[:, lanes]).astype(o_ref.dtype)

    @pl.when(jnp.logical_not(is_q | is_k))
    def _():
        o_ref[...] = jnp.dot(h_scr[...], w_ref[...], preferred_element_type=F32).astype(o_ref.dtype)


def project_even(x2, mod, layer, rows_per_seq, mod_row, g, w_stack, w_idx, wdt_t, qg_t, kg_t, cols, *, tm, tn):
    m, d = x2.shape
    n = cols["dt"]
    tm = min(tm, rows_per_seq if mod_row is None else m)
    tiles_per_seq = rows_per_seq // tm if mod_row is None else None
    row = (lambda i: i // tiles_per_seq) if mod_row is None else (lambda i: mod_row)
    na_width = NA_HEADS * NA_HEAD_DIM
    q_tiles = (cols["q"] // tn, (cols["q"] + na_width) // tn)
    k_tiles = (cols["k"] // tn, (cols["k"] + na_width) // tn)
    return pl.pallas_call(
        functools.partial(_proj_kernel, q_tiles=q_tiles, k_tiles=k_tiles),
        grid=(m // tm, n // tn),
        in_specs=[
            pl.BlockSpec((tm, d), lambda i, j: (i, 0)),
            pl.BlockSpec((None, None, 1, d), lambda i, j: (layer, row(i), 0, 0)),
            pl.BlockSpec((None, None, 1, d), lambda i, j: (layer, row(i), 0, 1)),
            pl.BlockSpec((1, d), lambda i, j: (0, 0)),
            pl.BlockSpec((None, d, tn), lambda i, j: (w_idx, 0, j)),
            pl.BlockSpec((wdt_t.shape[0], d), lambda i, j: (0, 0)),
            pl.BlockSpec((1, tn), lambda i, j: (0, 0)),
            pl.BlockSpec((1, tn), lambda i, j: (0, 0)),
        ],
        out_specs=[pl.BlockSpec((tm, tn), lambda i, j: (i, j)),
                   pl.BlockSpec((wdt_t.shape[0], tm), lambda i, j: (0, i))],
        out_shape=[jax.ShapeDtypeStruct((m, n), BF16),
                   jax.ShapeDtypeStruct((wdt_t.shape[0], m), F32)],
        scratch_shapes=[pltpu.VMEM((tm, d), BF16)],
        compiler_params=_params(("parallel", "arbitrary")),
        name="proj_dt",
    )(x2, mod, mod, g, w_stack, wdt_t, qg_t, kg_t)


def _bias_kernel(rpb_ref, o_ref):
    h = pl.program_id(0)
    qc = lax.broadcasted_iota(jnp.int32, (GRID_W, GRID_W), 0)
    kc = lax.broadcasted_iota(jnp.int32, (GRID_W, GRID_W), 1)
    start = jnp.clip(qc - NA_KW // 2, 0, GRID_W - NA_KW)
    in_win = (kc >= start) & (kc < start + NA_KW)
    dc = jnp.clip(kc - qc, -(NA_KW - 1), NA_KW - 1) + NA_KW - 1
    for dr in range(2 * NA_KH - 1):
        acc = jnp.zeros((GRID_W, GRID_W), F32)
        for j in range(2 * NA_KW - 1):
            acc = jnp.where(dc == j, rpb_ref[h, dr * (2 * NA_KW - 1) + j], acc)
        o_ref[dr] = jnp.where(in_win, acc, MASKED)


def expand_bias(rpb):
    nh = rpb.shape[0]
    ndr, ndc = 2 * NA_KH - 1, 2 * NA_KW - 1
    return pl.pallas_call(
        _bias_kernel,
        grid=(nh,),
        in_specs=[pl.BlockSpec(memory_space=pltpu.SMEM)],
        out_specs=pl.BlockSpec((None, ndr, GRID_W, GRID_W), lambda h: (h, 0, 0, 0)),
        out_shape=jax.ShapeDtypeStruct((nh, ndr, GRID_W, GRID_W), F32),
        compiler_params=_params(("arbitrary",)),
        name="rpb_expand",
    )(rpb.reshape(nh, ndr * ndc))


def _window_rows(case, qr, n_rows):
    if case == 1:
        lo, dr0 = qr, NA_KH - 1 - NA_KH // 2
    elif case == 0:
        lo = max(qr - NA_KH // 2, 0)
        dr0 = lo - qr + NA_KH - 1
    else:
        r = n_rows - NA_QROWS + qr
        rs = min(r - NA_KH // 2, n_rows - NA_KH)
        lo = rs - (n_rows - NA_KROWS)
        dr0 = rs - r + NA_KH - 1
    return lo, dr0


def _na_kernel(q_ref, gate_ref, k_ref, v_ref, kc_ref, vc_ref, cmat_ref, o_ref,
               v1_scr, vc1_scr, bias_scr, s0_scr, s1_scr, p0_scr, p1_scr, *, n_rows):
    seq = k_ref.shape[0]
    nctx = kc_ref.shape[0]
    dh = NA_HEAD_DIM
    nq = NA_QROWS * GRID_W
    nk = NA_KROWS * GRID_W
    nb = n_rows // NA_QROWS

    @pl.when(pl.program_id(1) == 0)
    def _():
        masked = jnp.full((GRID_W, GRID_W), MASKED, F32)
        for case in range(3):
            for qr in range(NA_QROWS):
                lo, dr0 = _window_rows(case, qr, n_rows)
                strip = [cmat_ref[dr0 + kr - lo] if lo <= kr < lo + NA_KH else masked
                         for kr in range(NA_KROWS)]
                bias_scr[case, qr * GRID_W:(qr + 1) * GRID_W, :] = jnp.concatenate(strip, axis=1)

    @pl.when((pl.program_id(0) == 0) & (pl.program_id(1) == 0))
    def _():
        v1_scr[:, dh:2 * dh] = jnp.ones((seq, dh), BF16)
        vc1_scr[:, dh:2 * dh] = jnp.ones((nctx, dh), BF16)

    v1_scr[:, 0:dh] = v_ref[...]
    vc1_scr[:, 0:dh] = vc_ref[...]

    nt = (((1,), (1,)), ((), ()))

    def block_rows(i):
        qrows = pl.ds(pl.multiple_of(i * nq, nq), nq)
        kstart = jnp.clip(i * NA_QROWS - NA_KH // 2, 0, n_rows - NA_KROWS)
        krows = pl.ds(pl.multiple_of(kstart * GRID_W, GRID_W), nk)
        return qrows, krows

    def scores(i, s_scr):
        i = jnp.minimum(i, nb - 1)
        qrows, krows = block_rows(i)
        qn = q_ref[qrows, :]
        case = jnp.where(i == 0, 0, jnp.where(i == nb - 1, 2, 1))
        s_scr[:, 0:nctx] = lax.dot_general(qn, kc_ref[...], nt, preferred_element_type=F32)
        s_scr[:, nctx:nctx + nk] = (lax.dot_general(qn, k_ref[krows, :], nt, preferred_element_type=F32)
                                    + bias_scr[case])

    def softmax(s_scr, p_scr):
        for g in range(nq // _NA_SOFTMAX_ROWS):
            rows = slice(g * _NA_SOFTMAX_ROWS, (g + 1) * _NA_SOFTMAX_ROWS)
            s = s_scr[rows, :]
            p_scr[rows, :] = jnp.exp(s - jnp.max(s, axis=-1, keepdims=True)).astype(BF16)

    def values(i, p_scr):
        qrows, krows = block_rows(i)
        o = jnp.dot(p_scr[:, 0:nctx], vc1_scr[...], preferred_element_type=F32)
        o += jnp.dot(p_scr[:, nctx:nctx + nk], v1_scr[krows, :], preferred_element_type=F32)
        gated = o[:, 0:dh] * (1.0 / o[:, dh:2 * dh]) * _silu(gate_ref[qrows, :].astype(F32))
        o_ref[qrows, :] = gated.astype(o_ref.dtype)

    scores(0, s0_scr)
    softmax(s0_scr, p0_scr)
    scores(1, s1_scr)

    def pair(jj, carry):
        j = 2 * jj
        values(j, p0_scr)
        softmax(s1_scr, p1_scr)
        scores(j + 2, s0_scr)
        values(j + 1, p1_scr)
        softmax(s0_scr, p0_scr)
        scores(j + 3, s1_scr)
        return carry

    lax.fori_loop(0, nb // 2, pair, 0)


def neighbourhood_attention(p, pc, cmat, cols):
    bsz, seq, _ = p.shape
    nctx = pc.shape[1]
    n_rows = seq // GRID_W
    nb = n_rows // NA_QROWS
    assert n_rows >= NA_KROWS and n_rows % NA_QROWS == 0 and nb % 2 == 0 and nctx % V7X_LANES == 0
    nq = NA_QROWS * GRID_W
    nk = NA_KROWS * GRID_W
    dh = NA_HEAD_DIM
    cq, cg, ck, cv = (cols[k] // dh for k in ("q", "gate", "k", "v"))
    slab = lambda c: pl.BlockSpec((None, seq, dh), lambda h, b: (b, 0, c + h))
    cslab = lambda c: pl.BlockSpec((None, nctx, dh), lambda h, b: (b, 0, c + h))
    return pl.pallas_call(
        functools.partial(_na_kernel, n_rows=n_rows),
        grid=(NA_HEADS, bsz),
        in_specs=[
            slab(cq), slab(cg), slab(ck), slab(cv), cslab(ck), cslab(cv),
            pl.BlockSpec((None, 2 * NA_KH - 1, GRID_W, GRID_W), lambda h, b: (h, 0, 0, 0)),
        ],
        out_specs=pl.BlockSpec((None, seq, dh), lambda h, b: (b, 0, h)),
        out_shape=jax.ShapeDtypeStruct((bsz, seq, NA_HEADS * dh), BF16),
        scratch_shapes=[
            pltpu.VMEM((seq, 2 * dh), BF16),
            pltpu.VMEM((nctx, 2 * dh), BF16),
            pltpu.VMEM((3, nq, nk), F32),
            pltpu.VMEM((nq, nctx + nk), F32),
            pltpu.VMEM((nq, nctx + nk), F32),
            pltpu.VMEM((nq, nctx + nk), BF16),
            pltpu.VMEM((nq, nctx + nk), BF16),
        ],
        compiler_params=_params(("arbitrary", "arbitrary")),
        name="nbr_attention",
    )(p, p, p, p, pc, pc, cmat)


def _ctx_attn_kernel(q_ref, gate_ref, k_ref, v_ref, o_ref):
    s = lax.dot_general(q_ref[...], k_ref[...], (((1,), (1,)), ((), ())), preferred_element_type=F32)
    m = jnp.max(s, axis=-1, keepdims=True)
    pr = jnp.exp(s - m)
    l = jnp.sum(pr, axis=-1, keepdims=True)
    o = jnp.dot(pr.astype(BF16), v_ref[...], preferred_element_type=F32)
    o_ref[...] = (o * (1.0 / l) * _silu(gate_ref[...].astype(F32))).astype(o_ref.dtype)


def context_attention(pc, cols):
    bsz, nctx, _ = pc.shape
    dh = NA_HEAD_DIM
    cq, cg, ck, cv = (cols[k] // dh for k in ("q", "gate", "k", "v"))
    blk = lambda c: pl.BlockSpec((None, nctx, dh), lambda b, h: (b, 0, c + h))
    return pl.pallas_call(
        _ctx_attn_kernel,
        grid=(bsz, NA_HEADS),
        in_specs=[blk(cq), blk(cg), blk(ck), blk(cv)],
        out_specs=pl.BlockSpec((None, nctx, dh), lambda b, h: (b, 0, h)),
        out_shape=jax.ShapeDtypeStruct((bsz, nctx, NA_HEADS * dh), BF16),
        compiler_params=_params(("parallel", "parallel")),
        name="ctx_attention",
    )(pc, pc, pc, pc)


_CONV_HALO = 16


def _conv_silu_chunk(src_ref, c, nc, w_ref, b_ref, stage_ref):
    t = SSD_CHUNK
    hl = _CONV_HALO
    length = src_ref.shape[0]
    start = pl.multiple_of(c * t, t)
    prev = src_ref[pl.ds(pl.multiple_of(jnp.maximum(start - hl, 0), hl), hl), :].astype(F32)
    nxt = src_ref[pl.ds(pl.multiple_of(jnp.minimum(start + t, length - hl), hl), hl), :].astype(F32)
    stage_ref[0:hl, :] = jnp.where(c > 0, prev, 0.0)
    stage_ref[hl:hl + t, :] = src_ref[pl.ds(start, t), :].astype(F32)
    stage_ref[hl + t:hl + t + hl, :] = jnp.where(c < nc - 1, nxt, 0.0)
    k = w_ref.shape[0]
    acc = jnp.zeros((t, src_ref.shape[1]), F32) + b_ref[...]
    for j in range(k):
        off = hl - k // 2 + j
        acc += stage_ref[off:off + t, :] * w_ref[j:j + 1, :]
    return _silu(acc)


def _ssd_kernel(x_ref, bm_ref, cm_ref, z_ref, dtt_ref, wx_ref, bx_ref, wb_ref, bb_ref, wc_ref, bc_ref,
                dtb_ref, a_ref, dskip_ref, ng_ref, h0_ref, y_ref, hfin_ref,
                xs_scr, bmc_scr, cmc_scr, yacc_scr, stx_scr, stb_scr, rows_scr, col_scr, fac_scr):
    t = SSD_CHUNK
    length = x_ref.shape[0]
    nc = length // t
    hp = SSD_HPG * SSD_HEAD_DIM

    def conv_chunk(c, side):
        rows = pl.ds(pl.multiple_of(c * t, t), t)
        xs_scr[rows, :] = _conv_silu_chunk(x_ref, c, nc, wx_ref, bx_ref, stx_scr.at[side]).astype(BF16)
        bmc_scr[rows, :] = _conv_silu_chunk(bm_ref, c, nc, wb_ref, bb_ref, stb_scr.at[2 * side]).astype(BF16)
        cmc_scr[rows, :] = _conv_silu_chunk(cm_ref, c, nc, wc_ref, bc_ref,
                                            stb_scr.at[2 * side + 1]).astype(BF16)

    conv_chunk(0, 0)
    conv_chunk(nc - 1, 1)
    hfin_ref[...] = h0_ref[...]

    ii = lax.broadcasted_iota(jnp.int32, (t, t), 0)
    jj = lax.broadcasted_iota(jnp.int32, (t, t), 1)
    lower = ii >= jj
    diag = ii == jj
    indicator = lambda cond: jnp.where(cond, 1.0, 0.0).astype(BF16)
    tri2_b = jnp.concatenate([indicator(ii <= jj), indicator(lower)], axis=1)
    fwd_rows = lax.broadcasted_iota(jnp.int32, (2 * SSD_HPG, t), 0) < SSD_HPG
    er = lax.broadcasted_iota(jnp.int32, (2 * SSD_HPG, hp), 0)
    el = lax.broadcasted_iota(jnp.int32, (2 * SSD_HPG, hp), 1) // SSD_HEAD_DIM
    expand = [indicator(er == el + d * SSD_HPG) for d in range(2)]
    lane_head = lax.broadcasted_iota(jnp.int32, (t, hp), 1) // SSD_HEAD_DIM
    head_lanes = [indicator(lane_head == r) for r in range(SSD_HPG)]
    nt = (((1,), (1,)), ((), ()))
    tn = (((0,), (0,)), ((), ()))

    def chunk_rows(c):
        return pl.ds(pl.multiple_of(c * t, t), t)

    def decay_terms(c):
        dt8 = _softplus(dtt_ref[:, chunk_rows(c)] + dtb_ref[...])
        dta8 = dt8 * a_ref[...]
        hi = dta8.astype(BF16).astype(F32)
        rest = dta8 - hi
        mid = rest.astype(BF16).astype(F32)
        terms = jnp.concatenate([hi, mid, rest - mid], axis=0).astype(BF16)
        cs3 = jnp.dot(terms, tri2_b, preferred_element_type=F32)
        cs2 = cs3[0:8] + cs3[8:16] + cs3[16:24]
        cs_row = jnp.where(fwd_rows, cs2[:, 0:t], cs2[:, t:2 * t])
        return dt8, cs_row, dt8.T, cs_row.T

    def factors(d, dt_col, cs_col):
        edge = t - 1 if d == 0 else 0
        to_end_col = jnp.exp(jnp.minimum(cs_col[edge:edge + 1, :] - cs_col, 0.0)) * dt_col
        ecs = jnp.dot(jnp.exp(cs_col).astype(BF16), expand[d], preferred_element_type=F32)
        to_end = jnp.dot(to_end_col.astype(BF16), expand[d], preferred_element_type=F32)
        return ecs, to_end

    def prepare_start(k):
        return decay_terms(jnp.minimum(k, nc - 1)), decay_terms(jnp.maximum(nc - 1 - k, 0))

    def prepare_park(terms, buf):
        (dt8, cs_row, dt_col, cs_col), (_, _, dt_col_b, cs_col_b) = terms
        rows_scr[buf, 0:8, :] = dt8
        rows_scr[buf, 8:16, :] = cs_row
        col_scr[buf] = cs_col
        ecs, to_end = factors(0, dt_col, cs_col)
        fac_scr[buf, 0] = ecs
        fac_scr[buf, 1] = to_end
        ecs_b, to_end_b = factors(1, dt_col_b, cs_col_b)
        fac_scr[buf, 2] = ecs_b
        fac_scr[buf, 3] = to_end_b

    def scan_step(k, buf, fill=lambda: None, fill2=lambda: None):
        rows_c, rows_p = chunk_rows(k), chunk_rows(nc - 1 - k)
        xs_c, bm_c, cm_c = xs_scr[rows_c, :], bmc_scr[rows_c, :], cmc_scr[rows_c, :]
        xs_p, bm_p, cm_p = xs_scr[rows_p, :], bmc_scr[rows_p, :], cmc_scr[rows_p, :]
        h_f, h_b = hfin_ref[0], hfin_ref[1]
        ecs_f, end_f, ecs_b, end_b = (fac_scr[buf, n] for n in range(4))
        cb = lax.dot_general(cm_c, bm_c, nt, preferred_element_type=F32)
        ys_f = jnp.dot(cm_c, h_f.astype(BF16), preferred_element_type=F32)
        ys_b = jnp.dot(cm_p, h_b.astype(BF16), preferred_element_type=F32)
        inc_f = lax.dot_general(bm_c, (xs_c.astype(F32) * end_f).astype(BF16), tn, preferred_element_type=F32)
        inc_b = lax.dot_general(bm_p, (xs_p.astype(F32) * end_b).astype(BF16), tn, preferred_element_type=F32)
        fill()
        hfin_ref[0] = h_f * ecs_f[t - 1:t, :] + inc_f
        hfin_ref[1] = h_b * ecs_b[0:1, :] + inc_b
        dt8 = rows_scr[buf, 0:8, :]
        cs_row = rows_scr[buf, 8:16, :]
        cs_col = col_scr[buf]
        cb_diag = jnp.where(diag, cb, 0.0)
        ws, xbd = [], []
        for r in range(SSD_HPG):
            f, b = r, SSD_HPG + r
            seg = jnp.where(lower, cs_col[:, f:f + 1] - cs_row[f:f + 1, :],
                            cs_col[:, b:b + 1] - cs_row[b:b + 1, :])
            dt_sel = jnp.where(lower, dt8[f:f + 1, :], dt8[b:b + 1, :])
            w = cb * (jnp.exp(seg) * dt_sel) + cb_diag * dt8[b:b + 1, :]
            ws.append(w.astype(BF16))
            xbd.append(xs_c * head_lanes[r])
        y_c = jnp.dot(jnp.concatenate(ws, axis=1), jnp.concatenate(xbd, axis=0), preferred_element_type=F32)
        fill2()
        return y_c + ys_f * ecs_f, ys_b * ecs_b

    def finish(c, y):
        rows = chunk_rows(c)
        y = y + yacc_scr[rows, :] + dskip_ref[...] * xs_scr[rows, :].astype(F32)
        yz = y * _silu(z_ref[rows, :].astype(F32))
        yz = yz * lax.rsqrt(jnp.mean(yz * yz, axis=-1, keepdims=True) + EPS)
        y_ref[rows, :] = (yz * ng_ref[...]).astype(y_ref.dtype)

    def first_half(k, buf):
        y_c, y_p = scan_step(k, buf, fill=lambda: conv_chunk(k + 1, 0),
                             fill2=lambda: conv_chunk(nc - 2 - k, 1))
        yacc_scr[chunk_rows(k), :] = y_c
        yacc_scr[chunk_rows(nc - 1 - k), :] = y_p
        prepare_park(prepare_start(k + 1), 1 - buf)

    def second_half(k, buf):
        y_c, y_p = scan_step(k, buf)
        finish(k, y_c)
        finish(nc - 1 - k, y_p)
        prepare_park(prepare_start(k + 1), 1 - buf)

    def walk(body, start, stop):
        if start % 2 == 0 and (stop - start) % 2 == 0 and stop - start > 2:
            def pair(kk, carry):
                body(2 * kk, 0)
                body(2 * kk + 1, 1)
                return carry

            lax.fori_loop(start // 2, stop // 2, pair, 0)
        else:
            for k in range(start, stop):
                body(jnp.int32(k), k % 2)

    prepare_park(prepare_start(0), 0)
    walk(first_half, 0, nc // 2)
    walk(second_half, nc // 2, nc)


def ssd_mixer(p, dt_t, conv_w, conv_b, dtb8, a8, dskip, norm_g, h0, cols):
    bsz, length, _ = p.shape
    hp = SSD_HPG * SSD_HEAD_DIM
    ns = SSD_STATE
    width = SSD_GROUPS * hp
    k = conv_w.shape[0]
    cx = cols["xbc"] // hp
    cb_ = (cols["xbc"] + width) // ns
    cc_ = (cols["xbc"] + width + SSD_GROUPS * ns) // ns
    cz = cols["z"] // hp
    st_rows = SSD_CHUNK + 2 * _CONV_HALO
    return pl.pallas_call(
        _ssd_kernel,
        grid=(bsz, SSD_GROUPS),
        in_specs=[
            pl.BlockSpec((None, length, hp), lambda b, g: (b, 0, cx + g)),
            pl.BlockSpec((None, length, ns), lambda b, g: (b, 0, cb_ + g)),
            pl.BlockSpec((None, length, ns), lambda b, g: (b, 0, cc_ + g)),
            pl.BlockSpec((None, length, hp), lambda b, g: (b, 0, cz + g)),
            pl.BlockSpec((2 * SSD_HPG, length), lambda b, g: (g, b)),
            pl.BlockSpec((k, hp), lambda b, g: (0, g)),
            pl.BlockSpec((1, hp), lambda b, g: (0, g)),
            pl.BlockSpec((k, ns), lambda b, g: (0, width // ns + g)),
            pl.BlockSpec((1, ns), lambda b, g: (0, width // ns + g)),
            pl.BlockSpec((k, ns), lambda b, g: (0, width // ns + SSD_GROUPS + g)),
            pl.BlockSpec((1, ns), lambda b, g: (0, width // ns + SSD_GROUPS + g)),
            pl.BlockSpec((None, 2 * SSD_HPG, V7X_LANES), lambda b, g: (g, 0, 0)),
            pl.BlockSpec((None, 2 * SSD_HPG, V7X_LANES), lambda b, g: (g, 0, 0)),
            pl.BlockSpec((1, hp), lambda b, g: (0, g)),
            pl.BlockSpec((1, hp), lambda b, g: (0, g)),
            pl.BlockSpec((None, None, 2, ns, hp), lambda b, g: (b, g, 0, 0, 0)),
        ],
        out_specs=[
            pl.BlockSpec((None, length, hp), lambda b, g: (b, 0, g)),
            pl.BlockSpec((None, None, 2, ns, hp), lambda b, g: (b, g, 0, 0, 0)),
        ],
        out_shape=[
            jax.ShapeDtypeStruct((bsz, length, width), BF16),
            jax.ShapeDtypeStruct((bsz, SSD_GROUPS, 2, ns, hp), F32),
        ],
        scratch_shapes=[
            pltpu.VMEM((length, hp), BF16),
            pltpu.VMEM((length, ns), BF16),
            pltpu.VMEM((length, ns), BF16),
            pltpu.VMEM((length, hp), F32),
            pltpu.VMEM((2, st_rows, hp), F32),
            pltpu.VMEM((4, st_rows, ns), F32),
            pltpu.VMEM((2, 4 * SSD_HPG, SSD_CHUNK), F32),
            pltpu.VMEM((2, SSD_CHUNK, 2 * SSD_HPG), F32),
            pltpu.VMEM((2, 4, SSD_CHUNK, hp), F32),
        ],
        compiler_params=_params(("parallel", "parallel")),
        name="ssd_mixer",
    )(p, p, p, p, dt_t, conv_w, conv_b, conv_w, conv_b, conv_w, conv_b, dtb8, a8, dskip, norm_g, h0)


def _out_kernel(*refs, n_in):
    a_refs, w_refs = refs[:n_in], refs[n_in:2 * n_in]
    x_ref, gate_ref, o_ref = refs[2 * n_in:]
    acc = jnp.dot(a_refs[0][...], w_refs[0][...].astype(BF16), preferred_element_type=F32)
    for a_ref, w_ref in zip(a_refs[1:], w_refs[1:]):
        acc += jnp.dot(a_ref[...], w_ref[...].astype(BF16), preferred_element_type=F32)
    o_ref[...] = x_ref[...] + gate_ref[...] * acc


def out_project(acts, w_stack, w_idx, x2, mod, layer, rows_per_seq, mod_row, *, tm, tn):
    m, d = x2.shape
    kw = acts[0].shape[1]
    assert all(a.shape[1] == kw for a in acts) and w_stack.shape[1] == kw * len(acts)
    w_out = w_stack
    tm = min(tm, rows_per_seq)
    tiles_per_seq = rows_per_seq // tm
    row = (lambda i: i // tiles_per_seq) if mod_row is None else (lambda i: mod_row)
    w_spec = lambda n: pl.BlockSpec((None, kw, tn), lambda i, j: (w_idx, n, j))
    return pl.pallas_call(
        functools.partial(_out_kernel, n_in=len(acts)),
        grid=(m // tm, d // tn),
        in_specs=[pl.BlockSpec((tm, kw), lambda i, j: (i, 0)) for _ in acts]
        + [w_spec(n) for n in range(len(acts))]
        + [pl.BlockSpec((tm, tn), lambda i, j: (i, j)),
           pl.BlockSpec((None, None, 1, tn), lambda i, j: (layer, row(i), 0, 2 * (d // tn) + j))],
        out_specs=pl.BlockSpec((tm, tn), lambda i, j: (i, j)),
        out_shape=jax.ShapeDtypeStruct((m, d), F32),
        compiler_params=_params(("parallel", "arbitrary")),
        name="out_proj",
    )(*acts, *([w_out] * len(acts)), x2, mod)


_SC_HALO = 16


def _short_conv_kernel(x_ref, xp_ref, xn_ref, shift_ref, scale_ref, g_ref, wb_ref, wc_ref, wh_ref, wg_ref,
                       cw_ref, z_ref, h_scr, u_scr, *, tiles_per_seq):
    tm = x_ref.shape[0]
    hl = _SC_HALO

    @pl.when(pl.program_id(1) == 0)
    def _():
        gain = g_ref[...] * (1.0 + scale_ref[...])
        shift = shift_ref[...]

        def norm(x):
            ms = jnp.mean(x * x, axis=-1, keepdims=True)
            return (x * lax.rsqrt(ms + EPS) * gain + shift).astype(BF16)

        h_scr[0:hl, :] = norm(xp_ref[...])
        h_scr[hl + tm:hl + tm + hl, :] = norm(xn_ref[...])

        def body(r, carry):
            rows = pl.multiple_of(r * _PROJ_ROWS, _PROJ_ROWS)
            h_scr[pl.ds(hl + rows, _PROJ_ROWS), :] = norm(x_ref[pl.ds(rows, _PROJ_ROWS), :])
            return carry

        lax.fori_loop(0, tm // _PROJ_ROWS, body, 0)

    t_in_seq = pl.program_id(0) % tiles_per_seq
    hs = h_scr[...]
    u = (jnp.dot(hs, wc_ref[...].astype(BF16), preferred_element_type=F32)
         * jnp.dot(hs, wh_ref[...].astype(BF16), preferred_element_type=F32))
    u_scr[0:hl, :] = jnp.where(t_in_seq > 0, u[0:hl], 0.0)
    u_scr[hl:hl + tm, :] = u[hl:hl + tm]
    u_scr[hl + tm:hl + tm + hl, :] = jnp.where(t_in_seq < tiles_per_seq - 1, u[hl + tm:], 0.0)
    k = cw_ref.shape[0]
    acc = u_scr[hl - k // 2:hl - k // 2 + tm, :] * cw_ref[0:1, :]
    for j in range(1, k):
        off = hl - k // 2 + j
        acc += u_scr[off:off + tm, :] * cw_ref[j:j + 1, :]
    hm = h_scr[hl:hl + tm, :]
    bg = jnp.dot(hm, wb_ref[...].astype(BF16), preferred_element_type=F32)
    gt = jnp.dot(hm, wg_ref[...].astype(BF16), preferred_element_type=F32)
    z_ref[...] = (_silu(gt) * bg * acc).astype(z_ref.dtype)


def short_conv_mixer(x2, mod, layer, rows_per_seq, mod_row, g, w_stack, w_idx, conv_w, *, tm, tc):
    m, d = x2.shape
    w_in = w_stack
    w = w_in.shape[2] // 4
    tm = min(tm, rows_per_seq)
    tiles_per_seq = rows_per_seq // tm
    row = (lambda i: i // tiles_per_seq) if mod_row is None else (lambda i: mod_row)
    hb = tm // _SC_HALO
    last = m // _SC_HALO - 1
    nblk = w // tc
    w_spec = lambda part: pl.BlockSpec((None, d, tc), lambda i, j: (w_idx, 0, part * nblk + j))
    return pl.pallas_call(
        functools.partial(_short_conv_kernel, tiles_per_seq=tiles_per_seq),
        grid=(m // tm, nblk),
        in_specs=[
            pl.BlockSpec((tm, d), lambda i, j: (i, 0)),
            pl.BlockSpec((_SC_HALO, d), lambda i, j: (jnp.maximum(i * hb - 1, 0), 0)),
            pl.BlockSpec((_SC_HALO, d), lambda i, j: (jnp.minimum((i + 1) * hb, last), 0)),
            pl.BlockSpec((None, None, 1, d), lambda i, j: (layer, row(i), 0, 0)),
            pl.BlockSpec((None, None, 1, d), lambda i, j: (layer, row(i), 0, 1)),
            pl.BlockSpec((1, d), lambda i, j: (0, 0)),
            w_spec(0), w_spec(1), w_spec(2), w_spec(3),
            pl.BlockSpec((conv_w.shape[0], tc), lambda i, j: (0, j)),
        ],
        out_specs=pl.BlockSpec((tm, tc), lambda i, j: (i, j)),
        out_shape=jax.ShapeDtypeStruct((m, w), BF16),
        scratch_shapes=[pltpu.VMEM((tm + 2 * _SC_HALO, d), BF16),
                        pltpu.VMEM((tm + 2 * _SC_HALO, tc), F32)],
        compiler_params=_params(("parallel", "arbitrary")),
        name="short_conv",
    )(x2, x2, x2, mod, mod, g, w_in, w_in, w_in, w_in, conv_w)


def _even_columns(d_model):
    na_width = NA_HEADS * NA_HEAD_DIM
    ssd_width = SSD_GROUPS * SSD_HPG * SSD_HEAD_DIM
    cols = {"q": 0}
    cols["gate"] = cols["q"] + na_width
    cols["z"] = cols["gate"] + na_width
    cols["k"] = cols["z"] + ssd_width
    cols["v"] = cols["k"] + na_width
    cols["xbc"] = cols["v"] + na_width
    cols["dt"] = cols["xbc"] + ssd_width + 2 * SSD_GROUPS * SSD_STATE
    return cols


def _dt_weight_rows(w_in, cols):
    wdt = w_in[:, cols["dt"]:cols["dt"] + 2 * SSD_HEADS]
    wdt = wdt.reshape(-1, 2, SSD_GROUPS, SSD_HPG).transpose(2, 1, 3, 0).reshape(2 * SSD_HEADS, -1)
    pad = jnp.zeros((V7X_LANES - 2 * SSD_HEADS, wdt.shape[1]), wdt.dtype)
    return jnp.concatenate([wdt, pad], axis=0).astype(BF16)


def _per_group_rows(v):
    v = v.astype(F32).reshape(2, SSD_GROUPS, SSD_HPG).transpose(1, 0, 2).reshape(SSD_GROUPS, 2 * SSD_HPG, 1)
    return jnp.broadcast_to(v, (SSD_GROUPS, 2 * SSD_HPG, V7X_LANES))


def _forward(x, c, ctx, c_ctx, ada_w, ada_b, norm_g, na_ssd_w_in, ssd_conv_w, ssd_conv_b, ssd_a_log,
             ssd_dt_bias, ssd_d, ssd_norm_g, q_norm_g, k_norm_g, na_rpb, na_ssd_w_out, sc_w_in,
             sc_conv_w, sc_w_out, *, tm, tn, tm_sc, tc, tm_out_even):
    bsz, seq, d = x.shape
    nctx = ctx.shape[1]
    depth = ada_w.shape[0]
    cols = _even_columns(d)
    hp = SSD_HPG * SSD_HEAD_DIM

    cond = jnp.concatenate([c, c_ctx[None, :], jnp.zeros((8 - bsz - 1, d), F32)], axis=0)
    mod = adaln_all(cond, ada_w, ada_b).reshape(depth, 8, 1, 3 * d)
    ctx_row = bsz

    w_out_even = na_ssd_w_out.astype(BF16)

    x2 = x.reshape(bsz * seq, d)
    ctx2 = ctx.reshape(bsz * nctx, d)
    for i in range(depth):
        update_ctx = any(j % 2 == 0 for j in range(i + 1, depth))
        g = norm_g[i].reshape(1, d)
        if i % 2 == 0:
            e = i // 2
            w_in = na_ssd_w_in[e]
            w_main = w_in[:, :cols["dt"]].astype(BF16)[None]
            wdt_t = _dt_weight_rows(w_in, cols)
            heads_per_tile = tn // NA_HEAD_DIM
            qg_t = jnp.tile(q_norm_g[e].astype(F32) * NA_HEAD_DIM ** -0.5, heads_per_tile).reshape(1, tn)
            kg_t = jnp.tile(k_norm_g[e].astype(F32), heads_per_tile).reshape(1, tn)
            p, dt_t = project_even(x2, mod, i, seq, None, g, w_main, 0, wdt_t, qg_t, kg_t, cols,
                                   tm=tm, tn=tn)
            pc, dtc_t = project_even(ctx2, mod, i, nctx, ctx_row, g, w_main, 0, wdt_t, qg_t, kg_t, cols,
                                     tm=tm, tn=tn)
            p3 = p.reshape(bsz, seq, -1)
            pc3 = pc.reshape(bsz, nctx, -1)
            cmat = expand_bias(na_rpb[e])
            ya = neighbourhood_attention(p3, pc3, cmat, cols)
            dtb8 = _per_group_rows(ssd_dt_bias[e])
            a8 = _per_group_rows(-jnp.exp(ssd_a_log[e].astype(F32)))
            dskip = jnp.repeat(ssd_d[e].astype(F32), SSD_HEAD_DIM).reshape(1, -1)
            sng = ssd_norm_g[e].reshape(1, -1)
            cw = ssd_conv_w[e]
            cb = ssd_conv_b[e].reshape(1, -1)
            h_zero = jnp.zeros((bsz, SSD_GROUPS, 2, SSD_STATE, hp), F32)
            ybc, h_ctx = ssd_mixer(pc3, dtc_t, cw, cb, dtb8, a8, dskip, sng, h_zero, cols)
            yb, _ = ssd_mixer(p3, dt_t, cw, cb, dtb8, a8, dskip, sng, h_ctx, cols)
            x2_new = out_project([ya.reshape(bsz * seq, -1), yb.reshape(bsz * seq, -1)], w_out_even, e,
                                 x2, mod, i, seq, None, tm=tm_out_even, tn=tn)
            if update_ctx:
                yac = context_attention(pc3, cols)
                ctx2 = out_project([yac.reshape(bsz * nctx, -1), ybc.reshape(bsz * nctx, -1)],
                                   w_out_even, e, ctx2, mod, i, nctx, ctx_row, tm=tm_out_even, tn=tn)
            x2 = x2_new
        else:
            o = i // 2
            z = short_conv_mixer(x2, mod, i, seq, None, g, sc_w_in, o, sc_conv_w[o], tm=tm_sc, tc=tc)
            x2_new = out_project([z], sc_w_out, o, x2, mod, i, seq, None, tm=tm, tn=tn)
            if update_ctx:
                zc = short_conv_mixer(ctx2, mod, i, nctx, ctx_row, g, sc_w_in, o, sc_conv_w[o],
                                      tm=tm_sc, tc=tc)
                ctx2 = out_project([zc], sc_w_out, o, ctx2, mod, i, nctx, ctx_row, tm=tm, tn=tn)
            x2 = x2_new
    return x2.reshape(bsz, seq, d)


def kernel(x, c, ctx, c_ctx, ada_w, ada_b, norm_g, na_ssd_w_in, ssd_conv_w, ssd_conv_b, ssd_a_log,
           ssd_dt_bias, ssd_d, ssd_norm_g, q_norm_g, k_norm_g, na_rpb, na_ssd_w_out, sc_w_in,
           sc_conv_w, sc_w_out):
    return _forward(x, c, ctx, c_ctx, ada_w, ada_b, norm_g, na_ssd_w_in, ssd_conv_w, ssd_conv_b,
                    ssd_a_log, ssd_dt_bias, ssd_d, ssd_norm_g, q_norm_g, k_norm_g, na_rpb,
                    na_ssd_w_out, sc_w_in, sc_conv_w, sc_w_out, tm=2048, tn=512, tm_sc=1024, tc=256, tm_out_even=1024)
```

```python
import functools

import jax
import jax.numpy as jnp
import numpy as np
from jax import lax
from jax.experimental import pallas as pl
from jax.experimental.pallas import tpu as pltpu

F32 = jnp.float32
BF16 = jnp.bfloat16

EPS = 1e-6
GRID_W = 64
NA_HEADS = 16
NA_HEAD_DIM = 128
NA_KH = 8
NA_KW = 16
SSD_HEAD_DIM = 64
SSD_GROUPS = 8
SSD_HPG = 4
SSD_HEADS = SSD_GROUPS * SSD_HPG
SSD_STATE = 128
SSD_CONV = 5
SSD_CHUNK = 128
SC_CONV = 3

V7X_LANES = 128
V7X_SUBLANES = 8
V7X_VMEM_LIMIT = 56 * 1024 * 1024

MASKED = -1e30

NA_QROWS = 8
NA_KROWS = 16
_NA_SOFTMAX_ROWS = 32


def _silu(x):
    return x * (1.0 / (1.0 + jnp.exp(-x)))


def _softplus(x):
    return jnp.maximum(x, 0.0) + jnp.log(1.0 + jnp.exp(-jnp.abs(x)))


def _params(sem, vmem=V7X_VMEM_LIMIT):
    return pltpu.CompilerParams(dimension_semantics=sem, vmem_limit_bytes=vmem)


def _adaln_kernel(cond_ref, w_ref, b_ref, o_ref):
    s = _silu(cond_ref[...])
    w = w_ref[...]
    rows = s.shape[0]
    s_hi = s.astype(BF16).astype(F32)
    s_terms = jnp.concatenate([s_hi, s - s_hi], axis=0).astype(BF16)
    w_hi = w.astype(BF16)
    w_mid = (w - w_hi.astype(F32)).astype(BF16)
    both = jnp.dot(s_terms, w_hi, preferred_element_type=F32)
    acc = both[0:rows] + both[rows:2 * rows]
    acc += jnp.dot(s_hi.astype(BF16), w_mid, preferred_element_type=F32)
    o_ref[...] = acc + b_ref[...]


def adaln_all(cond, ada_w, ada_b, tn=768):
    depth, d, n = ada_w.shape
    rows = cond.shape[0]
    return pl.pallas_call(
        _adaln_kernel,
        grid=(depth, n // tn),
        in_specs=[
            pl.BlockSpec((rows, d), lambda l, j: (0, 0)),
            pl.BlockSpec((None, d, tn), lambda l, j: (l, 0, j)),
            pl.BlockSpec((None, 1, tn), lambda l, j: (l, 0, j)),
        ],
        out_specs=pl.BlockSpec((None, rows, tn), lambda l, j: (l, 0, j)),
        out_shape=jax.ShapeDtypeStruct((depth, rows, n), F32),
        compiler_params=_params(("parallel", "parallel")),
        name="adaln",
    )(cond, ada_w, ada_b.reshape(depth, 1, n))


_PROJ_ROWS = 128


_HEAD_NORM_ROWS = 256


def _proj_kernel(x_ref, shift_ref, scale_ref, g_ref, w_ref, wdt_ref, qg_ref, kg_ref, o_ref, odt_ref, h_scr,
                 *, q_tiles, k_tiles):
    tm = x_ref.shape[0]
    tn = o_ref.shape[1]
    j = pl.program_id(1)

    @pl.when(j == 0)
    def _():
        gain = g_ref[...] * (1.0 + scale_ref[...])
        shift = shift_ref[...]

        def body(r, carry):
            rows = pl.ds(pl.multiple_of(r * _PROJ_ROWS, _PROJ_ROWS), _PROJ_ROWS)
            x = x_ref[rows, :]
            ms = jnp.mean(x * x, axis=-1, keepdims=True)
            h_scr[rows, :] = (x * lax.rsqrt(ms + EPS) * gain + shift).astype(BF16)
            return carry

        lax.fori_loop(0, tm // _PROJ_ROWS, body, 0)
        odt_ref[...] = lax.dot_general(wdt_ref[...], h_scr[...], (((1,), (1,)), ((), ())),
                                       preferred_element_type=F32)

    is_q = (j >= q_tiles[0]) & (j < q_tiles[1])
    is_k = (j >= k_tiles[0]) & (j < k_tiles[1])

    @pl.when(is_q | is_k)
    def _():
        acc = jnp.dot(h_scr[...], w_ref[...], preferred_element_type=F32)
        gain = jnp.where(is_q, qg_ref[...], kg_ref[...])
        step = min(_HEAD_NORM_ROWS, tm)
        for r in range(tm // step):
            for hb in range(tn // NA_HEAD_DIM):
                lanes = slice(hb * NA_HEAD_DIM, (hb + 1) * NA_HEAD_DIM)
                a = acc[r * step:(r + 1) * step, lanes]
                ms = jnp.mean(a * a, axis=-1, keepdims=True)
                o_ref[r * step:(r + 1) * step, lanes] = (a * lax.rsqrt(ms + EPS) * gain[:, lanes]).astype(o_ref.dtype)

    @pl.when(jnp.logical_not(is_q | is_k))
    def _():
        o_ref[...] = jnp.dot(h_scr[...], w_ref[...], preferred_element_type=F32).astype(o_ref.dtype)


def project_even(x2, mod, layer, rows_per_seq, mod_row, g, w_stack, w_idx, wdt_t, qg, kg, cols, *, tm, tn):
    m, d = x2.shape
    n = cols["dt"]
    tm = min(tm, rows_per_seq if mod_row is None else m)
    tiles_per_seq = rows_per_seq // tm if mod_row is None else None
    row = (lambda i: i // tiles_per_seq) if mod_row is None else (lambda i: mod_row)
    qg_t = jnp.tile(qg, tn // NA_HEAD_DIM).reshape(1, tn)
    kg_t = jnp.tile(kg, tn // NA_HEAD_DIM).reshape(1, tn)
    na_width = NA_HEADS * NA_HEAD_DIM
    q_tiles = (cols["q"] // tn, (cols["q"] + na_width) // tn)
    k_tiles = (cols["k"] // tn, (cols["k"] + na_width) // tn)
    return pl.pallas_call(
        functools.partial(_proj_kernel, q_tiles=q_tiles, k_tiles=k_tiles),
        grid=(m // tm, n // tn),
        in_specs=[
            pl.BlockSpec((tm, d), lambda i, j: (i, 0)),
            pl.BlockSpec((None, None, 1, d), lambda i, j: (layer, row(i), 0, 0)),
            pl.BlockSpec((None, None, 1, d), lambda i, j: (layer, row(i), 0, 1)),
            pl.BlockSpec((1, d), lambda i, j: (0, 0)),
            pl.BlockSpec((None, d, tn), lambda i, j: (w_idx, 0, j)),
            pl.BlockSpec((wdt_t.shape[0], d), lambda i, j: (0, 0)),
            pl.BlockSpec((1, tn), lambda i, j: (0, 0)),
            pl.BlockSpec((1, tn), lambda i, j: (0, 0)),
        ],
        out_specs=[pl.BlockSpec((tm, tn), lambda i, j: (i, j)),
                   pl.BlockSpec((wdt_t.shape[0], tm), lambda i, j: (0, i))],
        out_shape=[jax.ShapeDtypeStruct((m, n), BF16),
                   jax.ShapeDtypeStruct((wdt_t.shape[0], m), F32)],
        scratch_shapes=[pltpu.VMEM((tm, d), BF16)],
        compiler_params=_params(("parallel", "arbitrary")),
        name="proj_dt",
    )(x2, mod, mod, g, w_stack, wdt_t, qg_t, kg_t)


def _bias_kernel(rpb_ref, o_ref):
    h = pl.program_id(0)
    qc = lax.broadcasted_iota(jnp.int32, (GRID_W, GRID_W), 0)
    kc = lax.broadcasted_iota(jnp.int32, (GRID_W, GRID_W), 1)
    start = jnp.clip(qc - NA_KW // 2, 0, GRID_W - NA_KW)
    in_win = (kc >= start) & (kc < start + NA_KW)
    dc = jnp.clip(kc - qc, -(NA_KW - 1), NA_KW - 1) + NA_KW - 1
    for dr in range(2 * NA_KH - 1):
        acc = jnp.zeros((GRID_W, GRID_W), F32)
        for j in range(2 * NA_KW - 1):
            acc = jnp.where(dc == j, rpb_ref[h, dr * (2 * NA_KW - 1) + j], acc)
        o_ref[dr] = jnp.where(in_win, acc, MASKED)


def expand_bias(rpb):
    nh = rpb.shape[0]
    ndr, ndc = 2 * NA_KH - 1, 2 * NA_KW - 1
    return pl.pallas_call(
        _bias_kernel,
        grid=(nh,),
        in_specs=[pl.BlockSpec(memory_space=pltpu.SMEM)],
        out_specs=pl.BlockSpec((None, ndr, GRID_W, GRID_W), lambda h: (h, 0, 0, 0)),
        out_shape=jax.ShapeDtypeStruct((nh, ndr, GRID_W, GRID_W), F32),
        compiler_params=_params(("arbitrary",)),
        name="rpb_expand",
    )(rpb.reshape(nh, ndr * ndc))


def _window_rows(case, qr, n_rows):
    if case == 1:
        lo, dr0 = qr, NA_KH - 1 - NA_KH // 2
    elif case == 0:
        lo = max(qr - NA_KH // 2, 0)
        dr0 = lo - qr + NA_KH - 1
    else:
        r = n_rows - NA_QROWS + qr
        rs = min(r - NA_KH // 2, n_rows - NA_KH)
        lo = rs - (n_rows - NA_KROWS)
        dr0 = rs - r + NA_KH - 1
    return lo, dr0


def _na_kernel(q_ref, gate_ref, k_ref, v_ref, kc_ref, vc_ref, cmat_ref, o_ref,
               v1_scr, vc1_scr, bias_scr, s0_scr, s1_scr, p0_scr, p1_scr, *, n_rows):
    seq = k_ref.shape[0]
    nctx = kc_ref.shape[0]
    dh = NA_HEAD_DIM
    nq = NA_QROWS * GRID_W
    nk = NA_KROWS * GRID_W
    nb = n_rows // NA_QROWS

    @pl.when(pl.program_id(1) == 0)
    def _():
        masked = jnp.full((GRID_W, GRID_W), MASKED, F32)
        for case in range(3):
            for qr in range(NA_QROWS):
                lo, dr0 = _window_rows(case, qr, n_rows)
                strip = [cmat_ref[dr0 + kr - lo] if lo <= kr < lo + NA_KH else masked
                         for kr in range(NA_KROWS)]
                bias_scr[case, qr * GRID_W:(qr + 1) * GRID_W, :] = jnp.concatenate(strip, axis=1)

    @pl.when((pl.program_id(0) == 0) & (pl.program_id(1) == 0))
    def _():
        v1_scr[:, dh:2 * dh] = jnp.ones((seq, dh), BF16)
        vc1_scr[:, dh:2 * dh] = jnp.ones((nctx, dh), BF16)

    v1_scr[:, 0:dh] = v_ref[...]
    vc1_scr[:, 0:dh] = vc_ref[...]

    nt = (((1,), (1,)), ((), ()))

    def block_rows(i):
        qrows = pl.ds(pl.multiple_of(i * nq, nq), nq)
        kstart = jnp.clip(i * NA_QROWS - NA_KH // 2, 0, n_rows - NA_KROWS)
        krows = pl.ds(pl.multiple_of(kstart * GRID_W, GRID_W), nk)
        return qrows, krows

    def scores(i, s_scr):
        i = jnp.minimum(i, nb - 1)
        qrows, krows = block_rows(i)
        qn = q_ref[qrows, :]
        case = jnp.where(i == 0, 0, jnp.where(i == nb - 1, 2, 1))
        s_scr[:, 0:nctx] = lax.dot_general(qn, kc_ref[...], nt, preferred_element_type=F32)
        s_scr[:, nctx:nctx + nk] = (lax.dot_general(qn, k_ref[krows, :], nt, preferred_element_type=F32)
                                    + bias_scr[case])

    def softmax(s_scr, p_scr):
        for g in range(nq // _NA_SOFTMAX_ROWS):
            rows = slice(g * _NA_SOFTMAX_ROWS, (g + 1) * _NA_SOFTMAX_ROWS)
            s = s_scr[rows, :]
            p_scr[rows, :] = jnp.exp(s - jnp.max(s, axis=-1, keepdims=True)).astype(BF16)

    def values(i, p_scr):
        qrows, krows = block_rows(i)
        o = jnp.dot(p_scr[:, 0:nctx], vc1_scr[...], preferred_element_type=F32)
        o += jnp.dot(p_scr[:, nctx:nctx + nk], v1_scr[krows, :], preferred_element_type=F32)
        gated = o[:, 0:dh] * (1.0 / o[:, dh:2 * dh]) * _silu(gate_ref[qrows, :].astype(F32))
        o_ref[qrows, :] = gated.astype(o_ref.dtype)

    scores(0, s0_scr)
    softmax(s0_scr, p0_scr)
    scores(1, s1_scr)

    def pair(jj, carry):
        j = 2 * jj
        values(j, p0_scr)
        softmax(s1_scr, p1_scr)
        scores(j + 2, s0_scr)
        values(j + 1, p1_scr)
        softmax(s0_scr, p0_scr)
        scores(j + 3, s1_scr)
        return carry

    lax.fori_loop(0, nb // 2, pair, 0)


def neighbourhood_attention(p, pc, cmat, cols):
    bsz, seq, _ = p.shape
    nctx = pc.shape[1]
    n_rows = seq // GRID_W
    nb = n_rows // NA_QROWS
    assert n_rows >= NA_KROWS and n_rows % NA_QROWS == 0 and nb % 2 == 0 and nctx % V7X_LANES == 0
    nq = NA_QROWS * GRID_W
    nk = NA_KROWS * GRID_W
    dh = NA_HEAD_DIM
    cq, cg, ck, cv = (cols[k] // dh for k in ("q", "gate", "k", "v"))
    slab = lambda c: pl.BlockSpec((None, seq, dh), lambda h, b: (b, 0, c + h))
    cslab = lambda c: pl.BlockSpec((None, nctx, dh), lambda h, b: (b, 0, c + h))
    return pl.pallas_call(
        functools.partial(_na_kernel, n_rows=n_rows),
        grid=(NA_HEADS, bsz),
        in_specs=[
            slab(cq), slab(cg), slab(ck), slab(cv), cslab(ck), cslab(cv),
            pl.BlockSpec((None, 2 * NA_KH - 1, GRID_W, GRID_W), lambda h, b: (h, 0, 0, 0)),
        ],
        out_specs=pl.BlockSpec((None, seq, dh), lambda h, b: (b, 0, h)),
        out_shape=jax.ShapeDtypeStruct((bsz, seq, NA_HEADS * dh), BF16),
        scratch_shapes=[
            pltpu.VMEM((seq, 2 * dh), BF16),
            pltpu.VMEM((nctx, 2 * dh), BF16),
            pltpu.VMEM((3, nq, nk), F32),
            pltpu.VMEM((nq, nctx + nk), F32),
            pltpu.VMEM((nq, nctx + nk), F32),
            pltpu.VMEM((nq, nctx + nk), BF16),
            pltpu.VMEM((nq, nctx + nk), BF16),
        ],
        compiler_params=_params(("arbitrary", "arbitrary")),
        name="nbr_attention",
    )(p, p, p, p, pc, pc, cmat)


def _ctx_attn_kernel(q_ref, gate_ref, k_ref, v_ref, o_ref):
    s = lax.dot_general(q_ref[...], k_ref[...], (((1,), (1,)), ((), ())), preferred_element_type=F32)
    m = jnp.max(s, axis=-1, keepdims=True)
    pr = jnp.exp(s - m)
    l = jnp.sum(pr, axis=-1, keepdims=True)
    o = jnp.dot(pr.astype(BF16), v_ref[...], preferred_element_type=F32)
    o_ref[...] = (o * (1.0 / l) * _silu(gate_ref[...].astype(F32))).astype(o_ref.dtype)


def context_attention(pc, cols):
    bsz, nctx, _ = pc.shape
    dh = NA_HEAD_DIM
    cq, cg, ck, cv = (cols[k] // dh for k in ("q", "gate", "k", "v"))
    blk = lambda c: pl.BlockSpec((None, nctx, dh), lambda b, h: (b, 0, c + h))
    return pl.pallas_call(
        _ctx_attn_kernel,
        grid=(bsz, NA_HEADS),
        in_specs=[blk(cq), blk(cg), blk(ck), blk(cv)],
        out_specs=pl.BlockSpec((None, nctx, dh), lambda b, h: (b, 0, h)),
        out_shape=jax.ShapeDtypeStruct((bsz, nctx, NA_HEADS * dh), BF16),
        compiler_params=_params(("parallel", "parallel")),
        name="ctx_attention",
    )(pc, pc, pc, pc)


_CONV_HALO = 16


def _conv_silu_chunk(src_ref, c, nc, w_ref, b_ref, stage_ref):
    t = SSD_CHUNK
    hl = _CONV_HALO
    length = src_ref.shape[0]
    start = pl.multiple_of(c * t, t)
    prev = src_ref[pl.ds(pl.multiple_of(jnp.maximum(start - hl, 0), hl), hl), :].astype(F32)
    nxt = src_ref[pl.ds(pl.multiple_of(jnp.minimum(start + t, length - hl), hl), hl), :].astype(F32)
    stage_ref[0:hl, :] = jnp.where(c > 0, prev, 0.0)
    stage_ref[hl:hl + t, :] = src_ref[pl.ds(start, t), :].astype(F32)
    stage_ref[hl + t:hl + t + hl, :] = jnp.where(c < nc - 1, nxt, 0.0)
    k = w_ref.shape[0]
    acc = jnp.zeros((t, src_ref.shape[1]), F32) + b_ref[...]
    for j in range(k):
        off = hl - k // 2 + j
        acc += stage_ref[off:off + t, :] * w_ref[j:j + 1, :]
    return _silu(acc)


def _ssd_kernel(x_ref, bm_ref, cm_ref, z_ref, dtt_ref, wx_ref, bx_ref, wb_ref, bb_ref, wc_ref, bc_ref,
                dtb_ref, a_ref, dskip_ref, ng_ref, h0_ref, y_ref, hfin_ref,
                xs_scr, bmc_scr, cmc_scr, yacc_scr, stx_scr, stb_scr, rows_scr, col_scr, fac_scr):
    t = SSD_CHUNK
    length = x_ref.shape[0]
    nc = length // t
    hp = SSD_HPG * SSD_HEAD_DIM

    def conv_chunk(c, side):
        rows = pl.ds(pl.multiple_of(c * t, t), t)
        xs_scr[rows, :] = _conv_silu_chunk(x_ref, c, nc, wx_ref, bx_ref, stx_scr.at[side]).astype(BF16)
        bmc_scr[rows, :] = _conv_silu_chunk(bm_ref, c, nc, wb_ref, bb_ref, stb_scr.at[2 * side]).astype(BF16)
        cmc_scr[rows, :] = _conv_silu_chunk(cm_ref, c, nc, wc_ref, bc_ref,
                                            stb_scr.at[2 * side + 1]).astype(BF16)

    conv_chunk(0, 0)
    conv_chunk(nc - 1, 1)
    hfin_ref[...] = h0_ref[...]

    ii = lax.broadcasted_iota(jnp.int32, (t, t), 0)
    jj = lax.broadcasted_iota(jnp.int32, (t, t), 1)
    lower = ii >= jj
    diag = ii == jj
    indicator = lambda cond: jnp.where(cond, 1.0, 0.0).astype(BF16)
    tri2_b = jnp.concatenate([indicator(ii <= jj), indicator(lower)], axis=1)
    fwd_rows = lax.broadcasted_iota(jnp.int32, (2 * SSD_HPG, t), 0) < SSD_HPG
    er = lax.broadcasted_iota(jnp.int32, (2 * SSD_HPG, hp), 0)
    el = lax.broadcasted_iota(jnp.int32, (2 * SSD_HPG, hp), 1) // SSD_HEAD_DIM
    expand = [indicator(er == el + d * SSD_HPG) for d in range(2)]
    lane_head = lax.broadcasted_iota(jnp.int32, (t, hp), 1) // SSD_HEAD_DIM
    head_lanes = [indicator(lane_head == r) for r in range(SSD_HPG)]
    nt = (((1,), (1,)), ((), ()))
    tn = (((0,), (0,)), ((), ()))

    def chunk_rows(c):
        return pl.ds(pl.multiple_of(c * t, t), t)

    def decay_terms(c):
        dt8 = _softplus(dtt_ref[:, chunk_rows(c)] + dtb_ref[...])
        dta8 = dt8 * a_ref[...]
        hi = dta8.astype(BF16).astype(F32)
        rest = dta8 - hi
        mid = rest.astype(BF16).astype(F32)
        terms = jnp.concatenate([hi, mid, rest - mid], axis=0).astype(BF16)
        cs3 = jnp.dot(terms, tri2_b, preferred_element_type=F32)
        cs2 = cs3[0:8] + cs3[8:16] + cs3[16:24]
        cs_row = jnp.where(fwd_rows, cs2[:, 0:t], cs2[:, t:2 * t])
        return dt8, cs_row, dt8.T, cs_row.T

    def factors(d, dt_col, cs_col):
        edge = t - 1 if d == 0 else 0
        to_end_col = jnp.exp(jnp.minimum(cs_col[edge:edge + 1, :] - cs_col, 0.0)) * dt_col
        ecs = jnp.dot(jnp.exp(cs_col).astype(BF16), expand[d], preferred_element_type=F32)
        to_end = jnp.dot(to_end_col.astype(BF16), expand[d], preferred_element_type=F32)
        return ecs, to_end

    def prepare_start(k):
        return decay_terms(jnp.minimum(k, nc - 1)), decay_terms(jnp.maximum(nc - 1 - k, 0))

    def prepare_park(terms, buf):
        (dt8, cs_row, dt_col, cs_col), (_, _, dt_col_b, cs_col_b) = terms
        rows_scr[buf, 0:8, :] = dt8
        rows_scr[buf, 8:16, :] = cs_row
        col_scr[buf] = cs_col
        ecs, to_end = factors(0, dt_col, cs_col)
        fac_scr[buf, 0] = ecs
        fac_scr[buf, 1] = to_end
        ecs_b, to_end_b = factors(1, dt_col_b, cs_col_b)
        fac_scr[buf, 2] = ecs_b
        fac_scr[buf, 3] = to_end_b

    def scan_step(k, buf, fill=lambda: None):
        rows_c, rows_p = chunk_rows(k), chunk_rows(nc - 1 - k)
        xs_c, bm_c, cm_c = xs_scr[rows_c, :], bmc_scr[rows_c, :], cmc_scr[rows_c, :]
        xs_p, bm_p, cm_p = xs_scr[rows_p, :], bmc_scr[rows_p, :], cmc_scr[rows_p, :]
        h_f, h_b = hfin_ref[0], hfin_ref[1]
        ecs_f, end_f, ecs_b, end_b = (fac_scr[buf, n] for n in range(4))
        cb = lax.dot_general(cm_c, bm_c, nt, preferred_element_type=F32)
        ys_f = jnp.dot(cm_c, h_f.astype(BF16), preferred_element_type=F32)
        ys_b = jnp.dot(cm_p, h_b.astype(BF16), preferred_element_type=F32)
        inc_f = lax.dot_general(bm_c, (xs_c.astype(F32) * end_f).astype(BF16), tn, preferred_element_type=F32)
        inc_b = lax.dot_general(bm_p, (xs_p.astype(F32) * end_b).astype(BF16), tn, preferred_element_type=F32)
        fill()
        hfin_ref[0] = h_f * ecs_f[t - 1:t, :] + inc_f
        hfin_ref[1] = h_b * ecs_b[0:1, :] + inc_b
        dt8 = rows_scr[buf, 0:8, :]
        cs_row = rows_scr[buf, 8:16, :]
        cs_col = col_scr[buf]
        cb_diag = jnp.where(diag, cb, 0.0)
        ws, xbd = [], []
        for r in range(SSD_HPG):
            f, b = r, SSD_HPG + r
            seg = jnp.where(lower, cs_col[:, f:f + 1] - cs_row[f:f + 1, :],
                            cs_col[:, b:b + 1] - cs_row[b:b + 1, :])
            dt_sel = jnp.where(lower, dt8[f:f + 1, :], dt8[b:b + 1, :])
            w = cb * (jnp.exp(seg) * dt_sel) + cb_diag * dt8[b:b + 1, :]
            ws.append(w.astype(BF16))
            xbd.append(xs_c * head_lanes[r])
        y_c = jnp.dot(jnp.concatenate(ws, axis=1), jnp.concatenate(xbd, axis=0), preferred_element_type=F32)
        return y_c + ys_f * ecs_f, ys_b * ecs_b

    def finish(c, y):
        rows = chunk_rows(c)
        y = y + yacc_scr[rows, :] + dskip_ref[...] * xs_scr[rows, :].astype(F32)
        yz = y * _silu(z_ref[rows, :].astype(F32))
        yz = yz * lax.rsqrt(jnp.mean(yz * yz, axis=-1, keepdims=True) + EPS)
        y_ref[rows, :] = (yz * ng_ref[...]).astype(y_ref.dtype)

    def first_half(k, buf):
        y_c, y_p = scan_step(k, buf, fill=lambda: conv_chunk(k + 1, 0))
        yacc_scr[chunk_rows(k), :] = y_c
        yacc_scr[chunk_rows(nc - 1 - k), :] = y_p
        prepare_park(prepare_start(k + 1), 1 - buf)
        conv_chunk(nc - 2 - k, 1)

    def second_half(k, buf):
        y_c, y_p = scan_step(k, buf)
        finish(nc - 1 - k, y_p)
        finish(k, y_c)
        prepare_park(prepare_start(k + 1), 1 - buf)

    def walk(body, start, stop):
        if start % 2 == 0 and (stop - start) % 2 == 0 and stop - start > 2:
            def pair(kk, carry):
                body(2 * kk, 0)
                body(2 * kk + 1, 1)
                return carry

            lax.fori_loop(start // 2, stop // 2, pair, 0)
        else:
            for k in range(start, stop):
                body(jnp.int32(k), k % 2)

    prepare_park(prepare_start(0), 0)
    walk(first_half, 0, nc // 2)
    walk(second_half, nc // 2, nc)


def ssd_mixer(p, dt_t, conv_w, conv_b, dtb8, a8, dskip, norm_g, h0, cols):
    bsz, length, _ = p.shape
    hp = SSD_HPG * SSD_HEAD_DIM
    ns = SSD_STATE
    width = SSD_GROUPS * hp
    k = conv_w.shape[0]
    cx = cols["xbc"] // hp
    cb_ = (cols["xbc"] + width) // ns
    cc_ = (cols["xbc"] + width + SSD_GROUPS * ns) // ns
    cz = cols["z"] // hp
    st_rows = SSD_CHUNK + 2 * _CONV_HALO
    return pl.pallas_call(
        _ssd_kernel,
        grid=(bsz, SSD_GROUPS),
        in_specs=[
            pl.BlockSpec((None, length, hp), lambda b, g: (b, 0, cx + g)),
            pl.BlockSpec((None, length, ns), lambda b, g: (b, 0, cb_ + g)),
            pl.BlockSpec((None, length, ns), lambda b, g: (b, 0, cc_ + g)),
            pl.BlockSpec((None, length, hp), lambda b, g: (b, 0, cz + g)),
            pl.BlockSpec((2 * SSD_HPG, length), lambda b, g: (g, b)),
            pl.BlockSpec((k, hp), lambda b, g: (0, g)),
            pl.BlockSpec((1, hp), lambda b, g: (0, g)),
            pl.BlockSpec((k, ns), lambda b, g: (0, width // ns + g)),
            pl.BlockSpec((1, ns), lambda b, g: (0, width // ns + g)),
            pl.BlockSpec((k, ns), lambda b, g: (0, width // ns + SSD_GROUPS + g)),
            pl.BlockSpec((1, ns), lambda b, g: (0, width // ns + SSD_GROUPS + g)),
            pl.BlockSpec((None, 2 * SSD_HPG, V7X_LANES), lambda b, g: (g, 0, 0)),
            pl.BlockSpec((None, 2 * SSD_HPG, V7X_LANES), lambda b, g: (g, 0, 0)),
            pl.BlockSpec((1, hp), lambda b, g: (0, g)),
            pl.BlockSpec((1, hp), lambda b, g: (0, g)),
            pl.BlockSpec((None, None, 2, ns, hp), lambda b, g: (b, g, 0, 0, 0)),
        ],
        out_specs=[
            pl.BlockSpec((None, length, hp), lambda b, g: (b, 0, g)),
            pl.BlockSpec((None, None, 2, ns, hp), lambda b, g: (b, g, 0, 0, 0)),
        ],
        out_shape=[
            jax.ShapeDtypeStruct((bsz, length, width), BF16),
            jax.ShapeDtypeStruct((bsz, SSD_GROUPS, 2, ns, hp), F32),
        ],
        scratch_shapes=[
            pltpu.VMEM((length, hp), BF16),
            pltpu.VMEM((length, ns), BF16),
            pltpu.VMEM((length, ns), BF16),
            pltpu.VMEM((length, hp), F32),
            pltpu.VMEM((2, st_rows, hp), F32),
            pltpu.VMEM((4, st_rows, ns), F32),
            pltpu.VMEM((2, 4 * SSD_HPG, SSD_CHUNK), F32),
            pltpu.VMEM((2, SSD_CHUNK, 2 * SSD_HPG), F32),
            pltpu.VMEM((2, 4, SSD_CHUNK, hp), F32),
        ],
        compiler_params=_params(("parallel", "parallel")),
        name="ssd_mixer",
    )(p, p, p, p, dt_t, conv_w, conv_b, conv_w, conv_b, conv_w, conv_b, dtb8, a8, dskip, norm_g, h0)


def _out_kernel(*refs, n_in):
    a_refs, w_refs = refs[:n_in], refs[n_in:2 * n_in]
    x_ref, gate_ref, o_ref = refs[2 * n_in:]
    acc = jnp.dot(a_refs[0][...], w_refs[0][...].astype(BF16), preferred_element_type=F32)
    for a_ref, w_ref in zip(a_refs[1:], w_refs[1:]):
        acc += jnp.dot(a_ref[...], w_ref[...].astype(BF16), preferred_element_type=F32)
    o_ref[...] = x_ref[...] + gate_ref[...] * acc


def out_project(acts, w_stack, w_idx, x2, mod, layer, rows_per_seq, mod_row, *, tm, tn):
    m, d = x2.shape
    kw = acts[0].shape[1]
    assert all(a.shape[1] == kw for a in acts) and w_stack.shape[1] == kw * len(acts)
    w_out = w_stack
    tm = min(tm, rows_per_seq)
    tiles_per_seq = rows_per_seq // tm
    row = (lambda i: i // tiles_per_seq) if mod_row is None else (lambda i: mod_row)
    w_spec = lambda n: pl.BlockSpec((None, kw, tn), lambda i, j: (w_idx, n, j))
    return pl.pallas_call(
        functools.partial(_out_kernel, n_in=len(acts)),
        grid=(m // tm, d // tn),
        in_specs=[pl.BlockSpec((tm, kw), lambda i, j: (i, 0)) for _ in acts]
        + [w_spec(n) for n in range(len(acts))]
        + [pl.BlockSpec((tm, tn), lambda i, j: (i, j)),
           pl.BlockSpec((None, None, 1, tn), lambda i, j: (layer, row(i), 0, 2 * (d // tn) + j))],
        out_specs=pl.BlockSpec((tm, tn), lambda i, j: (i, j)),
        out_shape=jax.ShapeDtypeStruct((m, d), F32),
        compiler_params=_params(("parallel", "arbitrary")),
        name="out_proj",
    )(*acts, *([w_out] * len(acts)), x2, mod)


_SC_HALO = 16


def _short_conv_kernel(x_ref, xp_ref, xn_ref, shift_ref, scale_ref, g_ref, wb_ref, wc_ref, wh_ref, wg_ref,
                       cw_ref, z_ref, h_scr, u_scr, *, tiles_per_seq):
    tm = x_ref.shape[0]
    hl = _SC_HALO

    @pl.when(pl.program_id(1) == 0)
    def _():
        gain = g_ref[...] * (1.0 + scale_ref[...])
        shift = shift_ref[...]

        def norm(x):
            ms = jnp.mean(x * x, axis=-1, keepdims=True)
            return (x * lax.rsqrt(ms + EPS) * gain + shift).astype(BF16)

        h_scr[0:hl, :] = norm(xp_ref[...])
        h_scr[hl + tm:hl + tm + hl, :] = norm(xn_ref[...])

        def body(r, carry):
            rows = pl.multiple_of(r * _PROJ_ROWS, _PROJ_ROWS)
            h_scr[pl.ds(hl + rows, _PROJ_ROWS), :] = norm(x_ref[pl.ds(rows, _PROJ_ROWS), :])
            return carry

        lax.fori_loop(0, tm // _PROJ_ROWS, body, 0)

    t_in_seq = pl.program_id(0) % tiles_per_seq
    hs = h_scr[...]
    u = (jnp.dot(hs, wc_ref[...].astype(BF16), preferred_element_type=F32)
         * jnp.dot(hs, wh_ref[...].astype(BF16), preferred_element_type=F32))
    u_scr[0:hl, :] = jnp.where(t_in_seq > 0, u[0:hl], 0.0)
    u_scr[hl:hl + tm, :] = u[hl:hl + tm]
    u_scr[hl + tm:hl + tm + hl, :] = jnp.where(t_in_seq < tiles_per_seq - 1, u[hl + tm:], 0.0)
    k = cw_ref.shape[0]
    acc = u_scr[hl - k // 2:hl - k // 2 + tm, :] * cw_ref[0:1, :]
    for j in range(1, k):
        off = hl - k // 2 + j
        acc += u_scr[off:off + tm, :] * cw_ref[j:j + 1, :]
    hm = h_scr[hl:hl + tm, :]
    bg = jnp.dot(hm, wb_ref[...].astype(BF16), preferred_element_type=F32)
    gt = jnp.dot(hm, wg_ref[...].astype(BF16), preferred_element_type=F32)
    z_ref[...] = (_silu(gt) * bg * acc).astype(z_ref.dtype)


def short_conv_mixer(x2, mod, layer, rows_per_seq, mod_row, g, w_stack, w_idx, conv_w, *, tm, tc):
    m, d = x2.shape
    w_in = w_stack
    w = w_in.shape[2] // 4
    tm = min(tm, rows_per_seq)
    tiles_per_seq = rows_per_seq // tm
    row = (lambda i: i // tiles_per_seq) if mod_row is None else (lambda i: mod_row)
    hb = tm // _SC_HALO
    last = m // _SC_HALO - 1
    nblk = w // tc
    w_spec = lambda part: pl.BlockSpec((None, d, tc), lambda i, j: (w_idx, 0, part * nblk + j))
    return pl.pallas_call(
        functools.partial(_short_conv_kernel, tiles_per_seq=tiles_per_seq),
        grid=(m // tm, nblk),
        in_specs=[
            pl.BlockSpec((tm, d), lambda i, j: (i, 0)),
            pl.BlockSpec((_SC_HALO, d), lambda i, j: (jnp.maximum(i * hb - 1, 0), 0)),
            pl.BlockSpec((_SC_HALO, d), lambda i, j: (jnp.minimum((i + 1) * hb, last), 0)),
            pl.BlockSpec((None, None, 1, d), lambda i, j: (layer, row(i), 0, 0)),
            pl.BlockSpec((None, None, 1, d), lambda i, j: (layer, row(i), 0, 1)),
            pl.BlockSpec((1, d), lambda i, j: (0, 0)),
            w_spec(0), w_spec(1), w_spec(2), w_spec(3),
            pl.BlockSpec((conv_w.shape[0], tc), lambda i, j: (0, j)),
        ],
        out_specs=pl.BlockSpec((tm, tc), lambda i, j: (i, j)),
        out_shape=jax.ShapeDtypeStruct((m, w), BF16),
        scratch_shapes=[pltpu.VMEM((tm + 2 * _SC_HALO, d), BF16),
                        pltpu.VMEM((tm + 2 * _SC_HALO, tc), F32)],
        compiler_params=_params(("parallel", "arbitrary")),
        name="short_conv",
    )(x2, x2, x2, mod, mod, g, w_in, w_in, w_in, w_in, conv_w)


def _even_columns(d_model):
    na_width = NA_HEADS * NA_HEAD_DIM
    ssd_width = SSD_GROUPS * SSD_HPG * SSD_HEAD_DIM
    cols = {"q": 0}
    cols["gate"] = cols["q"] + na_width
    cols["z"] = cols["gate"] + na_width
    cols["k"] = cols["z"] + ssd_width
    cols["v"] = cols["k"] + na_width
    cols["xbc"] = cols["v"] + na_width
    cols["dt"] = cols["xbc"] + ssd_width + 2 * SSD_GROUPS * SSD_STATE
    return cols


def _dt_weight_rows(w_in, cols):
    wdt = w_in[:, cols["dt"]:cols["dt"] + 2 * SSD_HEADS]
    wdt = wdt.reshape(-1, 2, SSD_GROUPS, SSD_HPG).transpose(2, 1, 3, 0).reshape(2 * SSD_HEADS, -1)
    pad = jnp.zeros((V7X_LANES - 2 * SSD_HEADS, wdt.shape[1]), wdt.dtype)
    return jnp.concatenate([wdt, pad], axis=0).astype(BF16)


def _per_group_rows(v):
    v = v.astype(F32).reshape(2, SSD_GROUPS, SSD_HPG).transpose(1, 0, 2).reshape(SSD_GROUPS, 2 * SSD_HPG, 1)
    return jnp.broadcast_to(v, (SSD_GROUPS, 2 * SSD_HPG, V7X_LANES))


def _forward(x, c, ctx, c_ctx, ada_w, ada_b, norm_g, na_ssd_w_in, ssd_conv_w, ssd_conv_b, ssd_a_log,
             ssd_dt_bias, ssd_d, ssd_norm_g, q_norm_g, k_norm_g, na_rpb, na_ssd_w_out, sc_w_in,
             sc_conv_w, sc_w_out, *, tm, tn, tn_ctx, tm_sc, tc, tm_out_even):
    bsz, seq, d = x.shape
    nctx = ctx.shape[1]
    depth = ada_w.shape[0]
    cols = _even_columns(d)
    hp = SSD_HPG * SSD_HEAD_DIM

    cond = jnp.concatenate([c, c_ctx[None, :], jnp.zeros((V7X_SUBLANES - bsz - 1, d), F32)], axis=0)
    mod = adaln_all(cond, ada_w, ada_b).reshape(depth, V7X_SUBLANES, 1, 3 * d)
    ctx_row = bsz

    w_out_even = na_ssd_w_out.astype(BF16)

    x2 = x.reshape(bsz * seq, d)
    ctx2 = ctx.reshape(bsz * nctx, d)
    for i in range(depth):
        update_ctx = any(j % 2 == 0 for j in range(i + 1, depth))
        g = norm_g[i].reshape(1, d)
        if i % 2 == 0:
            e = i // 2
            w_in = na_ssd_w_in[e]
            w_main = w_in[:, :cols["dt"]].astype(BF16)[None]
            wdt_t = _dt_weight_rows(w_in, cols)
            qg = q_norm_g[e].astype(F32) * NA_HEAD_DIM ** -0.5
            kg = k_norm_g[e].astype(F32)
            p, dt_t = project_even(x2, mod, i, seq, None, g, w_main, 0, wdt_t, qg, kg, cols, tm=tm, tn=tn)
            pc, dtc_t = project_even(ctx2, mod, i, nctx, ctx_row, g, w_main, 0, wdt_t, qg, kg, cols,
                                     tm=tm, tn=tn_ctx)
            p3 = p.reshape(bsz, seq, -1)
            pc3 = pc.reshape(bsz, nctx, -1)
            cmat = expand_bias(na_rpb[e])
            ya = neighbourhood_attention(p3, pc3, cmat, cols)
            dtb8 = _per_group_rows(ssd_dt_bias[e])
            a8 = _per_group_rows(-jnp.exp(ssd_a_log[e].astype(F32)))
            dskip = jnp.repeat(ssd_d[e].astype(F32), SSD_HEAD_DIM).reshape(1, -1)
            sng = ssd_norm_g[e].reshape(1, -1)
            cw = ssd_conv_w[e]
            cb = ssd_conv_b[e].reshape(1, -1)
            h_zero = jnp.zeros((bsz, SSD_GROUPS, 2, SSD_STATE, hp), F32)
            ybc, h_ctx = ssd_mixer(pc3, dtc_t, cw, cb, dtb8, a8, dskip, sng, h_zero, cols)
            yb, _ = ssd_mixer(p3, dt_t, cw, cb, dtb8, a8, dskip, sng, h_ctx, cols)
            x2_new = out_project([ya.reshape(bsz * seq, -1), yb.reshape(bsz * seq, -1)], w_out_even, e,
                                 x2, mod, i, seq, None, tm=tm_out_even, tn=tn)
            if update_ctx:
                yac = context_attention(pc3, cols)
                ctx2 = out_project([yac.reshape(bsz * nctx, -1), ybc.reshape(bsz * nctx, -1)],
                                   w_out_even, e, ctx2, mod, i, nctx, ctx_row, tm=tm_out_even, tn=tn)
            x2 = x2_new
        else:
            o = i // 2
            z = short_conv_mixer(x2, mod, i, seq, None, g, sc_w_in, o, sc_conv_w[o], tm=tm_sc, tc=tc)
            x2_new = out_project([z], sc_w_out, o, x2, mod, i, seq, None, tm=tm, tn=tn)
            if update_ctx:
                zc = short_conv_mixer(ctx2, mod, i, nctx, ctx_row, g, sc_w_in, o, sc_conv_w[o],
                                      tm=tm_sc, tc=tc)
                ctx2 = out_project([zc], sc_w_out, o, ctx2, mod, i, nctx, ctx_row, tm=tm, tn=tn)
            x2 = x2_new
    return x2.reshape(bsz, seq, d)


def kernel(x, c, ctx, c_ctx, ada_w, ada_b, norm_g, na_ssd_w_in, ssd_conv_w, ssd_conv_b, ssd_a_log,
           ssd_dt_bias, ssd_d, ssd_norm_g, q_norm_g, k_norm_g, na_rpb, na_ssd_w_out, sc_w_in,
           sc_conv_w, sc_w_out):
    return _forward(x, c, ctx, c_ctx, ada_w, ada_b, norm_g, na_ssd_w_in, ssd_conv_w, ssd_conv_b,
                    ssd_a_log, ssd_dt_bias, ssd_d, ssd_norm_g, q_norm_g, k_norm_g, na_rpb,
                    na_ssd_w_out, sc_w_in, sc_conv_w, sc_w_out, tm=2048, tn=512, tn_ctx=1024, tm_sc=1024, tc=256,
                    tm_out_even=1024)
```

```python
import functools

import jax
import jax.numpy as jnp
import numpy as np
from jax import lax
from jax.experimental import pallas as pl
from jax.experimental.pallas import tpu as pltpu

F32 = jnp.float32
BF16 = jnp.bfloat16

EPS = 1e-6
GRID_W = 64
NA_HEADS = 16
NA_HEAD_DIM = 128
NA_KH = 8
NA_KW = 16
SSD_HEAD_DIM = 64
SSD_GROUPS = 8
SSD_HPG = 4
SSD_HEADS = SSD_GROUPS * SSD_HPG
SSD_STATE = 128
SSD_CONV = 5
SSD_CHUNK = 128
SC_CONV = 3

V7X_LANES = 128
V7X_SUBLANES = 8
V7X_VMEM_LIMIT = 56 * 1024 * 1024

MASKED = -1e30

NA_QROWS = 8
NA_KROWS = 16
_NA_SOFTMAX_ROWS = 32


def _silu(x):
    return x * (1.0 / (1.0 + jnp.exp(-x)))


def _softplus(x):
    return jnp.maximum(x, 0.0) + jnp.log(1.0 + jnp.exp(-jnp.abs(x)))


def _params(sem, vmem=V7X_VMEM_LIMIT):
    return pltpu.CompilerParams(dimension_semantics=sem, vmem_limit_bytes=vmem)


def _adaln_kernel(cond_ref, w_ref, b_ref, o_ref):
    s = _silu(cond_ref[...])
    w = w_ref[...]
    rows = s.shape[0]
    s_hi = s.astype(BF16).astype(F32)
    s_terms = jnp.concatenate([s_hi, s - s_hi], axis=0).astype(BF16)
    w_hi = w.astype(BF16)
    w_mid = (w - w_hi.astype(F32)).astype(BF16)
    both = jnp.dot(s_terms, w_hi, preferred_element_type=F32)
    acc = both[0:rows] + both[rows:2 * rows]
    acc += jnp.dot(s_hi.astype(BF16), w_mid, preferred_element_type=F32)
    o_ref[...] = acc + b_ref[...]


def adaln_all(cond, ada_w, ada_b, tn=768):
    depth, d, n = ada_w.shape
    rows = cond.shape[0]
    return pl.pallas_call(
        _adaln_kernel,
        grid=(depth, n // tn),
        in_specs=[
            pl.BlockSpec((rows, d), lambda l, j: (0, 0)),
            pl.BlockSpec((None, d, tn), lambda l, j: (l, 0, j)),
            pl.BlockSpec((None, 1, tn), lambda l, j: (l, 0, j)),
        ],
        out_specs=pl.BlockSpec((None, rows, tn), lambda l, j: (l, 0, j)),
        out_shape=jax.ShapeDtypeStruct((depth, rows, n), F32),
        compiler_params=_params(("parallel", "parallel")),
        name="adaln",
    )(cond, ada_w, ada_b.reshape(depth, 1, n))


_PROJ_ROWS = 128


_HEAD_NORM_ROWS = 256


def _proj_kernel(x_ref, shift_ref, scale_ref, g_ref, w_ref, wdt_ref, qg_ref, kg_ref, o_ref, odt_ref, h_scr,
                 *, q_tiles, k_tiles):
    tm = x_ref.shape[0]
    tn = o_ref.shape[1]
    j = pl.program_id(1)

    @pl.when(j == 0)
    def _():
        gain = g_ref[...] * (1.0 + scale_ref[...])
        shift = shift_ref[...]

        def body(r, carry):
            rows = pl.ds(pl.multiple_of(r * _PROJ_ROWS, _PROJ_ROWS), _PROJ_ROWS)
            x = x_ref[rows, :]
            ms = jnp.mean(x * x, axis=-1, keepdims=True)
            h_scr[rows, :] = (x * lax.rsqrt(ms + EPS) * gain + shift).astype(BF16)
            return carry

        lax.fori_loop(0, tm // _PROJ_ROWS, body, 0)
        odt_ref[...] = lax.dot_general(wdt_ref[...], h_scr[...], (((1,), (1,)), ((), ())),
                                       preferred_element_type=F32)

    is_q = (j >= q_tiles[0]) & (j < q_tiles[1])
    is_k = (j >= k_tiles[0]) & (j < k_tiles[1])

    @pl.when(is_q | is_k)
    def _():
        acc = jnp.dot(h_scr[...], w_ref[...], preferred_element_type=F32)
        gain = jnp.where(is_q, qg_ref[...], kg_ref[...])
        step = min(_HEAD_NORM_ROWS, tm)
        for r in range(tm // step):
            for hb in range(tn // NA_HEAD_DIM):
                lanes = slice(hb * NA_HEAD_DIM, (hb + 1) * NA_HEAD_DIM)
                a = acc[r * step:(r + 1) * step, lanes]
                ms = jnp.mean(a * a, axis=-1, keepdims=True)
                o_ref[r * step:(r + 1) * step, lanes] = (a * lax.rsqrt(ms + EPS) * gain[:, lanes]).astype(o_ref.dtype)

    @pl.when(jnp.logical_not(is_q | is_k))
    def _():
        o_ref[...] = jnp.dot(h_scr[...], w_ref[...], preferred_element_type=F32).astype(o_ref.dtype)


def project_even(x2, mod, layer, rows_per_seq, mod_row, g, w_stack, w_idx, wdt_t, qg, kg, cols, *, tm, tn):
    m, d = x2.shape
    n = cols["dt"]
    tm = min(tm, rows_per_seq if mod_row is None else m)
    tiles_per_seq = rows_per_seq // tm if mod_row is None else None
    row = (lambda i: i // tiles_per_seq) if mod_row is None else (lambda i: mod_row)
    qg_t = jnp.tile(qg, tn // NA_HEAD_DIM).reshape(1, tn)
    kg_t = jnp.tile(kg, tn // NA_HEAD_DIM).reshape(1, tn)
    na_width = NA_HEADS * NA_HEAD_DIM
    q_tiles = (cols["q"] // tn, (cols["q"] + na_width) // tn)
    k_tiles = (cols["k"] // tn, (cols["k"] + na_width) // tn)
    return pl.pallas_call(
        functools.partial(_proj_kernel, q_tiles=q_tiles, k_tiles=k_tiles),
        grid=(m // tm, n // tn),
        in_specs=[
            pl.BlockSpec((tm, d), lambda i, j: (i, 0)),
            pl.BlockSpec((None, None, 1, d), lambda i, j: (layer, row(i), 0, 0)),
            pl.BlockSpec((None, None, 1, d), lambda i, j: (layer, row(i), 0, 1)),
            pl.BlockSpec((1, d), lambda i, j: (0, 0)),
            pl.BlockSpec((None, d, tn), lambda i, j: (w_idx, 0, j)),
            pl.BlockSpec((wdt_t.shape[0], d), lambda i, j: (0, 0)),
            pl.BlockSpec((1, tn), lambda i, j: (0, 0)),
            pl.BlockSpec((1, tn), lambda i, j: (0, 0)),
        ],
        out_specs=[pl.BlockSpec((tm, tn), lambda i, j: (i, j)),
                   pl.BlockSpec((wdt_t.shape[0], tm), lambda i, j: (0, i))],
        out_shape=[jax.ShapeDtypeStruct((m, n), BF16),
                   jax.ShapeDtypeStruct((wdt_t.shape[0], m), F32)],
        scratch_shapes=[pltpu.VMEM((tm, d), BF16)],
        compiler_params=_params(("parallel", "arbitrary")),
        name="proj_dt",
    )(x2, mod, mod, g, w_stack, wdt_t, qg_t, kg_t)


def _bias_kernel(rpb_ref, o_ref):
    h = pl.program_id(0)
    qc = lax.broadcasted_iota(jnp.int32, (GRID_W, GRID_W), 0)
    kc = lax.broadcasted_iota(jnp.int32, (GRID_W, GRID_W), 1)
    start = jnp.clip(qc - NA_KW // 2, 0, GRID_W - NA_KW)
    in_win = (kc >= start) & (kc < start + NA_KW)
    dc = jnp.clip(kc - qc, -(NA_KW - 1), NA_KW - 1) + NA_KW - 1
    for dr in range(2 * NA_KH - 1):
        acc = jnp.zeros((GRID_W, GRID_W), F32)
        for j in range(2 * NA_KW - 1):
            acc = jnp.where(dc == j, rpb_ref[h, dr * (2 * NA_KW - 1) + j], acc)
        o_ref[dr] = jnp.where(in_win, acc, MASKED)


def expand_bias(rpb):
    nh = rpb.shape[0]
    ndr, ndc = 2 * NA_KH - 1, 2 * NA_KW - 1
    return pl.pallas_call(
        _bias_kernel,
        grid=(nh,),
        in_specs=[pl.BlockSpec(memory_space=pltpu.SMEM)],
        out_specs=pl.BlockSpec((None, ndr, GRID_W, GRID_W), lambda h: (h, 0, 0, 0)),
        out_shape=jax.ShapeDtypeStruct((nh, ndr, GRID_W, GRID_W), F32),
        compiler_params=_params(("arbitrary",)),
        name="rpb_expand",
    )(rpb.reshape(nh, ndr * ndc))


def _window_rows(case, qr, n_rows):
    if case == 1:
        lo, dr0 = qr, NA_KH - 1 - NA_KH // 2
    elif case == 0:
        lo = max(qr - NA_KH // 2, 0)
        dr0 = lo - qr + NA_KH - 1
    else:
        r = n_rows - NA_QROWS + qr
        rs = min(r - NA_KH // 2, n_rows - NA_KH)
        lo = rs - (n_rows - NA_KROWS)
        dr0 = rs - r + NA_KH - 1
    return lo, dr0


def _na_kernel(q_ref, gate_ref, k_ref, v_ref, kc_ref, vc_ref, cmat_ref, o_ref,
               v1_scr, vc1_scr, bias_scr, s0_scr, s1_scr, p0_scr, p1_scr, *, n_rows):
    seq = k_ref.shape[0]
    nctx = kc_ref.shape[0]
    dh = NA_HEAD_DIM
    nq = NA_QROWS * GRID_W
    nk = NA_KROWS * GRID_W
    nb = n_rows // NA_QROWS

    @pl.when(pl.program_id(1) == 0)
    def _():
        masked = jnp.full((GRID_W, GRID_W), MASKED, F32)
        for case in range(3):
            for qr in range(NA_QROWS):
                lo, dr0 = _window_rows(case, qr, n_rows)
                strip = [cmat_ref[dr0 + kr - lo] if lo <= kr < lo + NA_KH else masked
                         for kr in range(NA_KROWS)]
                bias_scr[case, qr * GRID_W:(qr + 1) * GRID_W, :] = jnp.concatenate(strip, axis=1)

    @pl.when((pl.program_id(0) == 0) & (pl.program_id(1) == 0))
    def _():
        v1_scr[:, dh:2 * dh] = jnp.ones((seq, dh), BF16)
        vc1_scr[:, dh:2 * dh] = jnp.ones((nctx, dh), BF16)

    v1_scr[:, 0:dh] = v_ref[...]
    vc1_scr[:, 0:dh] = vc_ref[...]

    nt = (((1,), (1,)), ((), ()))

    def block_rows(i):
        qrows = pl.ds(pl.multiple_of(i * nq, nq), nq)
        kstart = jnp.clip(i * NA_QROWS - NA_KH // 2, 0, n_rows - NA_KROWS)
        krows = pl.ds(pl.multiple_of(kstart * GRID_W, GRID_W), nk)
        return qrows, krows

    def scores(i, s_scr):
        i = jnp.minimum(i, nb - 1)
        qrows, krows = block_rows(i)
        qn = q_ref[qrows, :]
        case = jnp.where(i == 0, 0, jnp.where(i == nb - 1, 2, 1))
        s_scr[:, 0:nctx] = lax.dot_general(qn, kc_ref[...], nt, preferred_element_type=F32)
        s_scr[:, nctx:nctx + nk] = (lax.dot_general(qn, k_ref[krows, :], nt, preferred_element_type=F32)
                                    + bias_scr[case])

    def softmax(s_scr, p_scr):
        for g in range(nq // _NA_SOFTMAX_ROWS):
            rows = slice(g * _NA_SOFTMAX_ROWS, (g + 1) * _NA_SOFTMAX_ROWS)
            s = s_scr[rows, :]
            p_scr[rows, :] = jnp.exp(s - jnp.max(s, axis=-1, keepdims=True)).astype(BF16)

    def values(i, p_scr):
        qrows, krows = block_rows(i)
        o = jnp.dot(p_scr[:, 0:nctx], vc1_scr[...], preferred_element_type=F32)
        o += jnp.dot(p_scr[:, nctx:nctx + nk], v1_scr[krows, :], preferred_element_type=F32)
        gated = o[:, 0:dh] * (1.0 / o[:, dh:2 * dh]) * _silu(gate_ref[qrows, :].astype(F32))
        o_ref[qrows, :] = gated.astype(o_ref.dtype)

    scores(0, s0_scr)
    softmax(s0_scr, p0_scr)
    scores(1, s1_scr)

    def pair(jj, carry):
        j = 2 * jj
        values(j, p0_scr)
        softmax(s1_scr, p1_scr)
        scores(j + 2, s0_scr)
        values(j + 1, p1_scr)
        softmax(s0_scr, p0_scr)
        scores(j + 3, s1_scr)
        return carry

    lax.fori_loop(0, nb // 2, pair, 0)


def neighbourhood_attention(p, pc, cmat, cols):
    bsz, seq, _ = p.shape
    nctx = pc.shape[1]
    n_rows = seq // GRID_W
    nb = n_rows // NA_QROWS
    assert n_rows >= NA_KROWS and n_rows % NA_QROWS == 0 and nb % 2 == 0 and nctx % V7X_LANES == 0
    nq = NA_QROWS * GRID_W
    nk = NA_KROWS * GRID_W
    dh = NA_HEAD_DIM
    cq, cg, ck, cv = (cols[k] // dh for k in ("q", "gate", "k", "v"))
    slab = lambda c: pl.BlockSpec((None, seq, dh), lambda h, b: (b, 0, c + h))
    cslab = lambda c: pl.BlockSpec((None, nctx, dh), lambda h, b: (b, 0, c + h))
    return pl.pallas_call(
        functools.partial(_na_kernel, n_rows=n_rows),
        grid=(NA_HEADS, bsz),
        in_specs=[
            slab(cq), slab(cg), slab(ck), slab(cv), cslab(ck), cslab(cv),
            pl.BlockSpec((None, 2 * NA_KH - 1, GRID_W, GRID_W), lambda h, b: (h, 0, 0, 0)),
        ],
        out_specs=pl.BlockSpec((None, seq, dh), lambda h, b: (b, 0, h)),
        out_shape=jax.ShapeDtypeStruct((bsz, seq, NA_HEADS * dh), BF16),
        scratch_shapes=[
            pltpu.VMEM((seq, 2 * dh), BF16),
            pltpu.VMEM((nctx, 2 * dh), BF16),
            pltpu.VMEM((3, nq, nk), F32),
            pltpu.VMEM((nq, nctx + nk), F32),
            pltpu.VMEM((nq, nctx + nk), F32),
            pltpu.VMEM((nq, nctx + nk), BF16),
            pltpu.VMEM((nq, nctx + nk), BF16),
        ],
        compiler_params=_params(("arbitrary", "arbitrary")),
        name="nbr_attention",
    )(p, p, p, p, pc, pc, cmat)


def _ctx_attn_kernel(q_ref, gate_ref, k_ref, v_ref, o_ref):
    s = lax.dot_general(q_ref[...], k_ref[...], (((1,), (1,)), ((), ())), preferred_element_type=F32)
    m = jnp.max(s, axis=-1, keepdims=True)
    pr = jnp.exp(s - m)
    l = jnp.sum(pr, axis=-1, keepdims=True)
    o = jnp.dot(pr.astype(BF16), v_ref[...], preferred_element_type=F32)
    o_ref[...] = (o * (1.0 / l) * _silu(gate_ref[...].astype(F32))).astype(o_ref.dtype)


def context_attention(pc, cols):
    bsz, nctx, _ = pc.shape
    dh = NA_HEAD_DIM
    cq, cg, ck, cv = (cols[k] // dh for k in ("q", "gate", "k", "v"))
    blk = lambda c: pl.BlockSpec((None, nctx, dh), lambda b, h: (b, 0, c + h))
    return pl.pallas_call(
        _ctx_attn_kernel,
        grid=(bsz, NA_HEADS),
        in_specs=[blk(cq), blk(cg), blk(ck), blk(cv)],
        out_specs=pl.BlockSpec((None, nctx, dh), lambda b, h: (b, 0, h)),
        out_shape=jax.ShapeDtypeStruct((bsz, nctx, NA_HEADS * dh), BF16),
        compiler_params=_params(("parallel", "parallel")),
        name="ctx_attention",
    )(pc, pc, pc, pc)


_CONV_HALO = 16


def _conv_silu_chunk(src_ref, c, nc, w_ref, b_ref, stage_ref):
    t = SSD_CHUNK
    hl = _CONV_HALO
    length = src_ref.shape[0]
    start = pl.multiple_of(c * t, t)
    prev = src_ref[pl.ds(pl.multiple_of(jnp.maximum(start - hl, 0), hl), hl), :].astype(F32)
    nxt = src_ref[pl.ds(pl.multiple_of(jnp.minimum(start + t, length - hl), hl), hl), :].astype(F32)
    stage_ref[0:hl, :] = jnp.where(c > 0, prev, 0.0)
    stage_ref[hl:hl + t, :] = src_ref[pl.ds(start, t), :].astype(F32)
    stage_ref[hl + t:hl + t + hl, :] = jnp.where(c < nc - 1, nxt, 0.0)
    k = w_ref.shape[0]
    acc = jnp.zeros((t, src_ref.shape[1]), F32) + b_ref[...]
    for j in range(k):
        off = hl - k // 2 + j
        acc += stage_ref[off:off + t, :] * w_ref[j:j + 1, :]
    return _silu(acc)


def _ssd_kernel(x_ref, bm_ref, cm_ref, z_ref, dtt_ref, wx_ref, bx_ref, wb_ref, bb_ref, wc_ref, bc_ref,
                dtb_ref, a_ref, dskip_ref, ng_ref, h0_ref, y_ref, hfin_ref,
                xs_scr, bmc_scr, cmc_scr, yacc_scr, stx_scr, stb_scr, rows_scr, col_scr, fac_scr):
    t = SSD_CHUNK
    length = x_ref.shape[0]
    nc = length // t
    hp = SSD_HPG * SSD_HEAD_DIM

    def conv_chunk(c, side):
        rows = pl.ds(pl.multiple_of(c * t, t), t)
        xs_scr[rows, :] = _conv_silu_chunk(x_ref, c, nc, wx_ref, bx_ref, stx_scr.at[side]).astype(BF16)
        bmc_scr[rows, :] = _conv_silu_chunk(bm_ref, c, nc, wb_ref, bb_ref, stb_scr.at[2 * side]).astype(BF16)
        cmc_scr[rows, :] = _conv_silu_chunk(cm_ref, c, nc, wc_ref, bc_ref,
                                            stb_scr.at[2 * side + 1]).astype(BF16)

    conv_chunk(0, 0)
    conv_chunk(nc - 1, 1)
    hfin_ref[...] = h0_ref[...]

    ii = lax.broadcasted_iota(jnp.int32, (t, t), 0)
    jj = lax.broadcasted_iota(jnp.int32, (t, t), 1)
    lower = ii >= jj
    diag = ii == jj
    indicator = lambda cond: jnp.where(cond, 1.0, 0.0).astype(BF16)
    tri2_b = jnp.concatenate([indicator(ii <= jj), indicator(lower)], axis=1)
    fwd_rows = lax.broadcasted_iota(jnp.int32, (2 * SSD_HPG, t), 0) < SSD_HPG
    er = lax.broadcasted_iota(jnp.int32, (2 * SSD_HPG, hp), 0)
    el = lax.broadcasted_iota(jnp.int32, (2 * SSD_HPG, hp), 1) // SSD_HEAD_DIM
    expand = [indicator(er == el + d * SSD_HPG) for d in range(2)]
    lane_head = lax.broadcasted_iota(jnp.int32, (t, hp), 1) // SSD_HEAD_DIM
    head_lanes = [indicator(lane_head == r) for r in range(SSD_HPG)]
    nt = (((1,), (1,)), ((), ()))
    tn = (((0,), (0,)), ((), ()))

    def chunk_rows(c):
        return pl.ds(pl.multiple_of(c * t, t), t)

    def decay_terms(c):
        dt8 = _softplus(dtt_ref[:, chunk_rows(c)] + dtb_ref[...])
        dta8 = dt8 * a_ref[...]
        hi = dta8.astype(BF16).astype(F32)
        rest = dta8 - hi
        mid = rest.astype(BF16).astype(F32)
        terms = jnp.concatenate([hi, mid, rest - mid], axis=0).astype(BF16)
        cs3 = jnp.dot(terms, tri2_b, preferred_element_type=F32)
        cs2 = cs3[0:8] + cs3[8:16] + cs3[16:24]
        cs_row = jnp.where(fwd_rows, cs2[:, 0:t], cs2[:, t:2 * t])
        return dt8, cs_row, dt8.T, cs_row.T

    def factors(d, dt_col, cs_col):
        edge = t - 1 if d == 0 else 0
        to_end_col = jnp.exp(jnp.minimum(cs_col[edge:edge + 1, :] - cs_col, 0.0)) * dt_col
        ecs = jnp.dot(jnp.exp(cs_col).astype(BF16), expand[d], preferred_element_type=F32)
        to_end = jnp.dot(to_end_col.astype(BF16), expand[d], preferred_element_type=F32)
        return ecs, to_end

    def prepare_start(k):
        return decay_terms(jnp.minimum(k, nc - 1)), decay_terms(jnp.maximum(nc - 1 - k, 0))

    def prepare_park(terms, buf):
        (dt8, cs_row, dt_col, cs_col), (_, _, dt_col_b, cs_col_b) = terms
        rows_scr[buf, 0:8, :] = dt8
        rows_scr[buf, 8:16, :] = cs_row
        col_scr[buf] = cs_col
        ecs, to_end = factors(0, dt_col, cs_col)
        fac_scr[buf, 0] = ecs
        fac_scr[buf, 1] = to_end
        ecs_b, to_end_b = factors(1, dt_col_b, cs_col_b)
        fac_scr[buf, 2] = ecs_b
        fac_scr[buf, 3] = to_end_b

    def scan_step(k, buf, fill=lambda: None):
        rows_c, rows_p = chunk_rows(k), chunk_rows(nc - 1 - k)
        xs_c, bm_c, cm_c = xs_scr[rows_c, :], bmc_scr[rows_c, :], cmc_scr[rows_c, :]
        xs_p, bm_p, cm_p = xs_scr[rows_p, :], bmc_scr[rows_p, :], cmc_scr[rows_p, :]
        h_f, h_b = hfin_ref[0], hfin_ref[1]
        ecs_f, end_f, ecs_b, end_b = (fac_scr[buf, n] for n in range(4))
        cb = lax.dot_general(cm_c, bm_c, nt, preferred_element_type=F32)
        ys_f = jnp.dot(cm_c, h_f.astype(BF16), preferred_element_type=F32)
        ys_b = jnp.dot(cm_p, h_b.astype(BF16), preferred_element_type=F32)
        inc_f = lax.dot_general(bm_c, (xs_c.astype(F32) * end_f).astype(BF16), tn, preferred_element_type=F32)
        inc_b = lax.dot_general(bm_p, (xs_p.astype(F32) * end_b).astype(BF16), tn, preferred_element_type=F32)
        fill()
        hfin_ref[0] = h_f * ecs_f[t - 1:t, :] + inc_f
        hfin_ref[1] = h_b * ecs_b[0:1, :] + inc_b
        dt8 = rows_scr[buf, 0:8, :]
        cs_row = rows_scr[buf, 8:16, :]
        cs_col = col_scr[buf]
        cb_diag = jnp.where(diag, cb, 0.0)
        ws, xbd = [], []
        for r in range(SSD_HPG):
            f, b = r, SSD_HPG + r
            seg = jnp.where(lower, cs_col[:, f:f + 1] - cs_row[f:f + 1, :],
                            cs_col[:, b:b + 1] - cs_row[b:b + 1, :])
            dt_sel = jnp.where(lower, dt8[f:f + 1, :], dt8[b:b + 1, :])
            w = cb * (jnp.exp(seg) * dt_sel) + cb_diag * dt8[b:b + 1, :]
            ws.append(w.astype(BF16))
            xbd.append(xs_c * head_lanes[r])
        y_c = jnp.dot(jnp.concatenate(ws, axis=1), jnp.concatenate(xbd, axis=0), preferred_element_type=F32)
        return y_c + ys_f * ecs_f, ys_b * ecs_b

    def finish(c, y):
        rows = chunk_rows(c)
        y = y + yacc_scr[rows, :] + dskip_ref[...] * xs_scr[rows, :].astype(F32)
        yz = y * _silu(z_ref[rows, :].astype(F32))
        yz = yz * lax.rsqrt(jnp.mean(yz * yz, axis=-1, keepdims=True) + EPS)
        y_ref[rows, :] = (yz * ng_ref[...]).astype(y_ref.dtype)

    def first_half(k, buf):
        y_c, y_p = scan_step(k, buf, fill=lambda: conv_chunk(k + 1, 0))
        yacc_scr[chunk_rows(k), :] = y_c
        yacc_scr[chunk_rows(nc - 1 - k), :] = y_p
        prepare_park(prepare_start(k + 1), 1 - buf)
        conv_chunk(nc - 2 - k, 1)

    def second_half(k, buf):
        y_c, y_p = scan_step(k, buf)
        finish(nc - 1 - k, y_p)
        finish(k, y_c)
        prepare_park(prepare_start(k + 1), 1 - buf)

    def walk(body, start, stop):
        if start % 2 == 0 and (stop - start) % 2 == 0 and stop - start > 2:
            def pair(kk, carry):
                body(2 * kk, 0)
                body(2 * kk + 1, 1)
                return carry

            lax.fori_loop(start // 2, stop // 2, pair, 0)
        else:
            for k in range(start, stop):
                body(jnp.int32(k), k % 2)

    prepare_park(prepare_start(0), 0)
    walk(first_half, 0, nc // 2)
    walk(second_half, nc // 2, nc)


def ssd_mixer(p, dt_t, conv_w, conv_b, dtb8, a8, dskip, norm_g, h0, cols):
    bsz, length, _ = p.shape
    hp = SSD_HPG * SSD_HEAD_DIM
    ns = SSD_STATE
    width = SSD_GROUPS * hp
    k = conv_w.shape[0]
    cx = cols["xbc"] // hp
    cb_ = (cols["xbc"] + width) // ns
    cc_ = (cols["xbc"] + width + SSD_GROUPS * ns) // ns
    cz = cols["z"] // hp
    st_rows = SSD_CHUNK + 2 * _CONV_HALO
    return pl.pallas_call(
        _ssd_kernel,
        grid=(bsz, SSD_GROUPS),
        in_specs=[
            pl.BlockSpec((None, length, hp), lambda b, g: (b, 0, cx + g)),
            pl.BlockSpec((None, length, ns), lambda b, g: (b, 0, cb_ + g)),
            pl.BlockSpec((None, length, ns), lambda b, g: (b, 0, cc_ + g)),
            pl.BlockSpec((None, length, hp), lambda b, g: (b, 0, cz + g)),
            pl.BlockSpec((2 * SSD_HPG, length), lambda b, g: (g, b)),
            pl.BlockSpec((k, hp), lambda b, g: (0, g)),
            pl.BlockSpec((1, hp), lambda b, g: (0, g)),
            pl.BlockSpec((k, ns), lambda b, g: (0, width // ns + g)),
            pl.BlockSpec((1, ns), lambda b, g: (0, width // ns + g)),
            pl.BlockSpec((k, ns), lambda b, g: (0, width // ns + SSD_GROUPS + g)),
            pl.BlockSpec((1, ns), lambda b, g: (0, width // ns + SSD_GROUPS + g)),
            pl.BlockSpec((None, 2 * SSD_HPG, V7X_LANES), lambda b, g: (g, 0, 0)),
            pl.BlockSpec((None, 2 * SSD_HPG, V7X_LANES), lambda b, g: (g, 0, 0)),
            pl.BlockSpec((1, hp), lambda b, g: (0, g)),
            pl.BlockSpec((1, hp), lambda b, g: (0, g)),
            pl.BlockSpec((None, None, 2, ns, hp), lambda b, g: (b, g, 0, 0, 0)),
        ],
        out_specs=[
            pl.BlockSpec((None, length, hp), lambda b, g: (b, 0, g)),
            pl.BlockSpec((None, None, 2, ns, hp), lambda b, g: (b, g, 0, 0, 0)),
        ],
        out_shape=[
            jax.ShapeDtypeStruct((bsz, length, width), BF16),
            jax.ShapeDtypeStruct((bsz, SSD_GROUPS, 2, ns, hp), F32),
        ],
        scratch_shapes=[
            pltpu.VMEM((length, hp), BF16),
            pltpu.VMEM((length, ns), BF16),
            pltpu.VMEM((length, ns), BF16),
            pltpu.VMEM((length, hp), F32),
            pltpu.VMEM((2, st_rows, hp), F32),
            pltpu.VMEM((4, st_rows, ns), F32),
            pltpu.VMEM((2, 4 * SSD_HPG, SSD_CHUNK), F32),
            pltpu.VMEM((2, SSD_CHUNK, 2 * SSD_HPG), F32),
            pltpu.VMEM((2, 4, SSD_CHUNK, hp), F32),
        ],
        compiler_params=_params(("parallel", "parallel")),
        name="ssd_mixer",
    )(p, p, p, p, dt_t, conv_w, conv_b, conv_w, conv_b, conv_w, conv_b, dtb8, a8, dskip, norm_g, h0)


def _out_kernel(*refs, n_in):
    a_refs, w_refs = refs[:n_in], refs[n_in:2 * n_in]
    x_ref, gate_ref, o_ref = refs[2 * n_in:]
    acc = jnp.dot(a_refs[0][...], w_refs[0][...].astype(BF16), preferred_element_type=F32)
    for a_ref, w_ref in zip(a_refs[1:], w_refs[1:]):
        acc += jnp.dot(a_ref[...], w_ref[...].astype(BF16), preferred_element_type=F32)
    o_ref[...] = x_ref[...] + gate_ref[...] * acc


def out_project(acts, w_stack, w_idx, x2, mod, layer, rows_per_seq, mod_row, *, tm, tn):
    m, d = x2.shape
    kw = acts[0].shape[1]
    assert all(a.shape[1] == kw for a in acts) and w_stack.shape[1] == kw * len(acts)
    w_out = w_stack
    tm = min(tm, rows_per_seq)
    tiles_per_seq = rows_per_seq // tm
    row = (lambda i: i // tiles_per_seq) if mod_row is None else (lambda i: mod_row)
    w_spec = lambda n: pl.BlockSpec((None, kw, tn), lambda i, j: (w_idx, n, j))
    return pl.pallas_call(
        functools.partial(_out_kernel, n_in=len(acts)),
        grid=(m // tm, d // tn),
        in_specs=[pl.BlockSpec((tm, kw), lambda i, j: (i, 0)) for _ in acts]
        + [w_spec(n) for n in range(len(acts))]
        + [pl.BlockSpec((tm, tn), lambda i, j: (i, j)),
           pl.BlockSpec((None, None, 1, tn), lambda i, j: (layer, row(i), 0, 2 * (d // tn) + j))],
        out_specs=pl.BlockSpec((tm, tn), lambda i, j: (i, j)),
        out_shape=jax.ShapeDtypeStruct((m, d), F32),
        compiler_params=_params(("parallel", "arbitrary")),
        name="out_proj",
    )(*acts, *([w_out] * len(acts)), x2, mod)


_SC_HALO = 16


def _short_conv_kernel(x_ref, xp_ref, xn_ref, shift_ref, scale_ref, g_ref, wb_ref, wc_ref, wh_ref, wg_ref,
                       cw_ref, z_ref, h_scr, u_scr, *, tiles_per_seq):
    tm = x_ref.shape[0]
    hl = _SC_HALO

    @pl.when(pl.program_id(1) == 0)
    def _():
        gain = g_ref[...] * (1.0 + scale_ref[...])
        shift = shift_ref[...]

        def norm(x):
            ms = jnp.mean(x * x, axis=-1, keepdims=True)
            return (x * lax.rsqrt(ms + EPS) * gain + shift).astype(BF16)

        h_scr[0:hl, :] = norm(xp_ref[...])
        h_scr[hl + tm:hl + tm + hl, :] = norm(xn_ref[...])

        def body(r, carry):
            rows = pl.multiple_of(r * _PROJ_ROWS, _PROJ_ROWS)
            h_scr[pl.ds(hl + rows, _PROJ_ROWS), :] = norm(x_ref[pl.ds(rows, _PROJ_ROWS), :])
            return carry

        lax.fori_loop(0, tm // _PROJ_ROWS, body, 0)

    t_in_seq = pl.program_id(0) % tiles_per_seq
    hs = h_scr[...]
    u = (jnp.dot(hs, wc_ref[...].astype(BF16), preferred_element_type=F32)
         * jnp.dot(hs, wh_ref[...].astype(BF16), preferred_element_type=F32))
    u_scr[0:hl, :] = jnp.where(t_in_seq > 0, u[0:hl], 0.0)
    u_scr[hl:hl + tm, :] = u[hl:hl + tm]
    u_scr[hl + tm:hl + tm + hl, :] = jnp.where(t_in_seq < tiles_per_seq - 1, u[hl + tm:], 0.0)
    k = cw_ref.shape[0]
    acc = u_scr[hl - k // 2:hl - k // 2 + tm, :] * cw_ref[0:1, :]
    for j in range(1, k):
        off = hl - k // 2 + j
        acc += u_scr[off:off + tm, :] * cw_ref[j:j + 1, :]
    hm = h_scr[hl:hl + tm, :]
    bg = jnp.dot(hm, wb_ref[...].astype(BF16), preferred_element_type=F32)
    gt = jnp.dot(hm, wg_ref[...].astype(BF16), preferred_element_type=F32)
    z_ref[...] = (_silu(gt) * bg * acc).astype(z_ref.dtype)


def short_conv_mixer(x2, mod, layer, rows_per_seq, mod_row, g, w_stack, w_idx, conv_w, *, tm, tc):
    m, d = x2.shape
    w_in = w_stack
    w = w_in.shape[2] // 4
    tm = min(tm, rows_per_seq)
    tiles_per_seq = rows_per_seq // tm
    row = (lambda i: i // tiles_per_seq) if mod_row is None else (lambda i: mod_row)
    hb = tm // _SC_HALO
    last = m // _SC_HALO - 1
    nblk = w // tc
    w_spec = lambda part: pl.BlockSpec((None, d, tc), lambda i, j: (w_idx, 0, part * nblk + j))
    return pl.pallas_call(
        functools.partial(_short_conv_kernel, tiles_per_seq=tiles_per_seq),
        grid=(m // tm, nblk),
        in_specs=[
            pl.BlockSpec((tm, d), lambda i, j: (i, 0), pipeline_mode=pl.Buffered(1)),
            pl.BlockSpec((_SC_HALO, d), lambda i, j: (jnp.maximum(i * hb - 1, 0), 0)),
            pl.BlockSpec((_SC_HALO, d), lambda i, j: (jnp.minimum((i + 1) * hb, last), 0)),
            pl.BlockSpec((None, None, 1, d), lambda i, j: (layer, row(i), 0, 0)),
            pl.BlockSpec((None, None, 1, d), lambda i, j: (layer, row(i), 0, 1)),
            pl.BlockSpec((1, d), lambda i, j: (0, 0)),
            w_spec(0), w_spec(1), w_spec(2), w_spec(3),
            pl.BlockSpec((conv_w.shape[0], tc), lambda i, j: (0, j)),
        ],
        out_specs=pl.BlockSpec((tm, tc), lambda i, j: (i, j)),
        out_shape=jax.ShapeDtypeStruct((m, w), BF16),
        scratch_shapes=[pltpu.VMEM((tm + 2 * _SC_HALO, d), BF16),
                        pltpu.VMEM((tm + 2 * _SC_HALO, tc), F32)],
        compiler_params=_params(("parallel", "arbitrary")),
        name="short_conv",
    )(x2, x2, x2, mod, mod, g, w_in, w_in, w_in, w_in, conv_w)


def _even_columns(d_model):
    na_width = NA_HEADS * NA_HEAD_DIM
    ssd_width = SSD_GROUPS * SSD_HPG * SSD_HEAD_DIM
    cols = {"q": 0}
    cols["gate"] = cols["q"] + na_width
    cols["z"] = cols["gate"] + na_width
    cols["k"] = cols["z"] + ssd_width
    cols["v"] = cols["k"] + na_width
    cols["xbc"] = cols["v"] + na_width
    cols["dt"] = cols["xbc"] + ssd_width + 2 * SSD_GROUPS * SSD_STATE
    return cols


def _dt_weight_rows(w_in, cols):
    wdt = w_in[:, cols["dt"]:cols["dt"] + 2 * SSD_HEADS]
    wdt = wdt.reshape(-1, 2, SSD_GROUPS, SSD_HPG).transpose(2, 1, 3, 0).reshape(2 * SSD_HEADS, -1)
    pad = jnp.zeros((V7X_LANES - 2 * SSD_HEADS, wdt.shape[1]), wdt.dtype)
    return jnp.concatenate([wdt, pad], axis=0).astype(BF16)


def _per_group_rows(v):
    v = v.astype(F32).reshape(2, SSD_GROUPS, SSD_HPG).transpose(1, 0, 2).reshape(SSD_GROUPS, 2 * SSD_HPG, 1)
    return jnp.broadcast_to(v, (SSD_GROUPS, 2 * SSD_HPG, V7X_LANES))


def _forward(x, c, ctx, c_ctx, ada_w, ada_b, norm_g, na_ssd_w_in, ssd_conv_w, ssd_conv_b, ssd_a_log,
             ssd_dt_bias, ssd_d, ssd_norm_g, q_norm_g, k_norm_g, na_rpb, na_ssd_w_out, sc_w_in,
             sc_conv_w, sc_w_out, *, tm, tn, tn_ctx, tm_sc, tc, tm_out_even):
    bsz, seq, d = x.shape
    nctx = ctx.shape[1]
    depth = ada_w.shape[0]
    cols = _even_columns(d)
    hp = SSD_HPG * SSD_HEAD_DIM

    cond = jnp.concatenate([c, c_ctx[None, :], jnp.zeros((V7X_SUBLANES - bsz - 1, d), F32)], axis=0)
    mod = adaln_all(cond, ada_w, ada_b).reshape(depth, V7X_SUBLANES, 1, 3 * d)
    ctx_row = bsz

    w_out_even = na_ssd_w_out.astype(BF16)

    x2 = x.reshape(bsz * seq, d)
    ctx2 = ctx.reshape(bsz * nctx, d)
    for i in range(depth):
        update_ctx = any(j % 2 == 0 for j in range(i + 1, depth))
        g = norm_g[i].reshape(1, d)
        if i % 2 == 0:
            e = i // 2
            w_in = na_ssd_w_in[e]
            w_main = w_in[:, :cols["dt"]].astype(BF16)[None]
            wdt_t = _dt_weight_rows(w_in, cols)
            qg = q_norm_g[e].astype(F32) * NA_HEAD_DIM ** -0.5
            kg = k_norm_g[e].astype(F32)
            p, dt_t = project_even(x2, mod, i, seq, None, g, w_main, 0, wdt_t, qg, kg, cols, tm=tm, tn=tn)
            pc, dtc_t = project_even(ctx2, mod, i, nctx, ctx_row, g, w_main, 0, wdt_t, qg, kg, cols,
                                     tm=tm, tn=tn_ctx)
            p3 = p.reshape(bsz, seq, -1)
            pc3 = pc.reshape(bsz, nctx, -1)
            cmat = expand_bias(na_rpb[e])
            ya = neighbourhood_attention(p3, pc3, cmat, cols)
            dtb8 = _per_group_rows(ssd_dt_bias[e])
            a8 = _per_group_rows(-jnp.exp(ssd_a_log[e].astype(F32)))
            dskip = jnp.repeat(ssd_d[e].astype(F32), SSD_HEAD_DIM).reshape(1, -1)
            sng = ssd_norm_g[e].reshape(1, -1)
            cw = ssd_conv_w[e]
            cb = ssd_conv_b[e].reshape(1, -1)
            h_zero = jnp.zeros((bsz, SSD_GROUPS, 2, SSD_STATE, hp), F32)
            ybc, h_ctx = ssd_mixer(pc3, dtc_t, cw, cb, dtb8, a8, dskip, sng, h_zero, cols)
            yb, _ = ssd_mixer(p3, dt_t, cw, cb, dtb8, a8, dskip, sng, h_ctx, cols)
            x2_new = out_project([ya.reshape(bsz * seq, -1), yb.reshape(bsz * seq, -1)], w_out_even, e,
                                 x2, mod, i, seq, None, tm=tm_out_even, tn=tn)
            if update_ctx:
                yac = context_attention(pc3, cols)
                ctx2 = out_project([yac.reshape(bsz * nctx, -1), ybc.reshape(bsz * nctx, -1)],
                                   w_out_even, e, ctx2, mod, i, nctx, ctx_row, tm=tm_out_even, tn=tn)
            x2 = x2_new
        else:
            o = i // 2
            z = short_conv_mixer(x2, mod, i, seq, None, g, sc_w_in, o, sc_conv_w[o], tm=tm_sc, tc=tc)
            x2_new = out_project([z], sc_w_out, o, x2, mod, i, seq, None, tm=tm, tn=tn)
            if update_ctx:
                zc = short_conv_mixer(ctx2, mod, i, nctx, ctx_row, g, sc_w_in, o, sc_conv_w[o],
                                      tm=tm_sc, tc=tc)
                ctx2 = out_project([zc], sc_w_out, o, ctx2, mod, i, nctx, ctx_row, tm=tm, tn=tn)
            x2 = x2_new
    return x2.reshape(bsz, seq, d)


def kernel(x, c, ctx, c_ctx, ada_w, ada_b, norm_g, na_ssd_w_in, ssd_conv_w, ssd_conv_b, ssd_a_log,
           ssd_dt_bias, ssd_d, ssd_norm_g, q_norm_g, k_norm_g, na_rpb, na_ssd_w_out, sc_w_in,
           sc_conv_w, sc_w_out):
    return _forward(x, c, ctx, c_ctx, ada_w, ada_b, norm_g, na_ssd_w_in, ssd_conv_w, ssd_conv_b,
                    ssd_a_log, ssd_dt_bias, ssd_d, ssd_norm_g, q_norm_g, k_norm_g, na_rpb,
                    na_ssd_w_out, sc_w_in, sc_conv_w, sc_w_out, tm=2048, tn=512, tn_ctx=1024, tm_sc=2048, tc=256,
                    tm_out_even=1024)
```
